```python
import math
import jax, jax.numpy as jnp
from jax import lax
import numpy as np

D_MODEL = 1024
BATCH = 8
SEQ = 4096
DEPTH = 1

GRID_W = 64
CTX_LEN = 256
MIX_WIDTH = D_MODEL
ATT_WIDTH = MIX_WIDTH // 2
POOL_WIDTH = MIX_WIDTH - ATT_WIDTH
ATT_HEADS = 4
ATT_HEAD_DIM = ATT_WIDTH // ATT_HEADS
QK_DIM = ATT_HEAD_DIM // 2
POOL_WINDOWS = (2, 4, 8, 16)
POOL_GROUPS = len(POOL_WINDOWS)
POOL_GROUP_DIM = POOL_WIDTH // POOL_GROUPS
IN_WIDTH = 3 * ATT_WIDTH + POOL_WIDTH
D_FF = ((8 * D_MODEL // 3 + 255) // 256) * 256
N_MOD = 9
ROPE_BASE = 10000.0
Q_BLOCK = 128
EPS = 1e-6

kernel_name = "hybrid_diffattn_pool_macaron_dit_layer"


def _lambda_init(layer):
    return 0.8 - 0.6 * math.exp(-0.3 * layer)


def rms_norm(x, g):
    xf = x.astype(jnp.float32)
    y = xf * lax.rsqrt(jnp.mean(xf * xf, axis=-1, keepdims=True) + EPS)
    return (y * g.astype(jnp.float32)).astype(x.dtype)


def adaln_params(cond, w, b):
    m = (jax.nn.silu(cond) @ w + b).reshape(-1, 1, N_MOD, D_MODEL)
    return [m[:, :, i] for i in range(N_MOD)]


def modulate(h, shift, scale):
    return h * (1 + scale) + shift


def swiglu(h, w_in, w_out):
    gate, up = jnp.split(h @ w_in, 2, axis=-1)
    return (jax.nn.silu(gate) * up) @ w_out


def axial_rope_tables(n):
    rows = n // GRID_W
    row = jnp.repeat(jnp.arange(rows, dtype=jnp.float32), GRID_W)
    col = jnp.tile(jnp.arange(GRID_W, dtype=jnp.float32), rows)
    nf = QK_DIM // 4
    freqs = ROPE_BASE ** (-jnp.arange(nf, dtype=jnp.float32) / nf)
    ar = row[:, None] * freqs
    ac = col[:, None] * freqs
    ang = jnp.concatenate([ar, ar, ac, ac], axis=-1)
    return jnp.cos(ang), jnp.sin(ang)


def apply_axial_rope(t, cos, sin):
    tf = t.astype(jnp.float32)
    tr = tf.reshape(tf.shape[:-1] + (2, 2, QK_DIM // 4))
    rot = jnp.stack([-tr[..., 1, :], tr[..., 0, :]], axis=-2).reshape(tf.shape)
    c = cos[None, :, None, None, :]
    s = sin[None, :, None, None, :]
    return (tf * c + rot * s).astype(t.dtype)


def split_qk(t):
    b, n, _ = t.shape
    return t.reshape(b, n, ATT_HEADS, 2, QK_DIM)


def split_v(t):
    b, n, _ = t.shape
    return t.reshape(b, n, ATT_HEADS, ATT_HEAD_DIM).transpose(0, 2, 1, 3)


def diff_attention(q, k, v, lam, g_sub, lam_init):
    s = jnp.einsum('bhmqd,bhmkd->bhmqk', q, k).astype(jnp.float32) * (QK_DIM ** -0.5)
    p = jax.nn.softmax(s, axis=-1)
    a = p[:, :, 0] - lam * p[:, :, 1]
    o = jnp.einsum('bhqk,bhkv->bhqv', a.astype(v.dtype), v)
    return rms_norm(o, g_sub) * (1.0 - lam_init)


def multiscale_pool(u, w_pool, pool_scale):
    b, n, _ = u.shape
    uf = u.astype(jnp.float32)
    csum = jnp.concatenate([jnp.zeros((b, 1, POOL_WIDTH), jnp.float32), jnp.cumsum(uf, axis=1)], axis=1)
    t = jnp.arange(n)
    outs = []
    for gi, w in enumerate(POOL_WINDOWS):
        lo = w // 2
        hi = w - w // 2 - 1
        start = jnp.maximum(t - lo, 0)
        end = jnp.minimum(t + hi, n - 1)
        sl = slice(gi * POOL_GROUP_DIM, (gi + 1) * POOL_GROUP_DIM)
        seg = csum[..., sl]
        total = jnp.take(seg, end + 1, axis=1) - jnp.take(seg, start, axis=1)
        cnt = (end - start + 1).astype(jnp.float32)
        diff = (total / cnt[None, :, None] - uf[..., sl]).astype(u.dtype)
        outs.append(diff @ w_pool[gi])
    return jnp.concatenate(outs, axis=-1) * pool_scale


def setup_inputs(seed: int = 0) -> dict:
    key = jax.random.key(seed)
    ks = jax.random.split(key, 24)
    f32 = jnp.float32
    nrm = lambda k, shape, s: jax.random.normal(k, shape, f32) * s
    gain = lambda k, shape: 1.0 + 0.05 * jax.random.normal(k, shape, f32)
    L, D = DEPTH, D_MODEL
    return {
        "x": nrm(ks[0], (BATCH, SEQ, D), 1.0),
        "c": nrm(ks[1], (BATCH, D), 1.0),
        "ctx": nrm(ks[2], (BATCH, CTX_LEN, D), 1.0),
        "c_ctx": nrm(ks[3], (D,), 1.0),
        "w_mod": nrm(ks[4], (L, D, N_MOD * D), 0.5 * D ** -0.5),
        "b_mod": nrm(ks[5], (L, N_MOD * D), 0.01),
        "g_ffn1": gain(ks[6], (L, D)),
        "ffn1_w_in": nrm(ks[7], (L, D, 2 * D_FF), D ** -0.5),
        "ffn1_w_out": nrm(ks[8], (L, D_FF, D), D_FF ** -0.5),
        "g_mix": gain(ks[9], (L, D)),
        "w_in": nrm(ks[10], (L, D, IN_WIDTH), D ** -0.5),
        "lambda_q1": nrm(ks[11], (L, QK_DIM), 0.1),
        "lambda_k1": nrm(ks[12], (L, QK_DIM), 0.1),
        "lambda_q2": nrm(ks[13], (L, QK_DIM), 0.1),
        "lambda_k2": nrm(ks[14], (L, QK_DIM), 0.1),
        "g_sub": gain(ks[15], (L, ATT_HEAD_DIM)),
        "w_pool": nrm(ks[16], (L, POOL_GROUPS, POOL_GROUP_DIM, POOL_GROUP_DIM), POOL_GROUP_DIM ** -0.5),
        "pool_scale": gain(ks[17], (L, POOL_WIDTH)),
        "w_out": nrm(ks[18], (L, MIX_WIDTH, D), MIX_WIDTH ** -0.5),
        "g_ffn2": gain(ks[19], (L, D)),
        "ffn2_w_in": nrm(ks[20], (L, D, 2 * D_FF), D ** -0.5),
        "ffn2_w_out": nrm(ks[21], (L, D_FF, D), D_FF ** -0.5),
        "g_final": gain(ks[22], (D,)),
    }


def reference(x, c, ctx, c_ctx, w_mod, b_mod, g_ffn1, ffn1_w_in, ffn1_w_out, g_mix, w_in,
              lambda_q1, lambda_k1, lambda_q2, lambda_k2, g_sub, w_pool, pool_scale, w_out,
              g_ffn2, ffn2_w_in, ffn2_w_out, g_final):
    b, n, _ = x.shape
    nblk = n // Q_BLOCK
    cos, sin = axial_rope_tables(n)
    cx = ctx
    for l in range(DEPTH):
        last = l == DEPTH - 1
        sh1, sc1, gt1, sh2, sc2, gt2, sh3, sc3, gt3 = adaln_params(c, w_mod[l], b_mod[l])
        ch1, cc1, cg1, ch2, cc2, cg2, ch3, cc3, cg3 = adaln_params(c_ctx, w_mod[l], b_mod[l])

        x = x + 0.5 * gt1 * swiglu(modulate(rms_norm(x, g_ffn1[l]), sh1, sc1), ffn1_w_in[l], ffn1_w_out[l])
        cx = cx + 0.5 * cg1 * swiglu(modulate(rms_norm(cx, g_ffn1[l]), ch1, cc1), ffn1_w_in[l], ffn1_w_out[l])

        hx = modulate(rms_norm(x, g_mix[l]), sh2, sc2) @ w_in[l]
        hc = modulate(rms_norm(cx, g_mix[l]), ch2, cc2) @ w_in[l]
        qx, kx, vx, ux = jnp.split(hx, [ATT_WIDTH, 2 * ATT_WIDTH, 3 * ATT_WIDTH], axis=-1)
        qc, kc, vc, uc = jnp.split(hc, [ATT_WIDTH, 2 * ATT_WIDTH, 3 * ATT_WIDTH], axis=-1)

        lam_init = _lambda_init(l)
        lam = (jnp.exp(jnp.sum(lambda_q1[l].astype(jnp.float32) * lambda_k1[l].astype(jnp.float32)))
               - jnp.exp(jnp.sum(lambda_q2[l].astype(jnp.float32) * lambda_k2[l].astype(jnp.float32)))
               + lam_init)

        q_lat = apply_axial_rope(split_qk(qx), cos, sin).transpose(0, 2, 3, 1, 4)
        k_lat = apply_axial_rope(split_qk(kx), cos, sin).transpose(0, 2, 3, 1, 4)
        q_ctx = split_qk(qc).transpose(0, 2, 3, 1, 4)
        k_ctx = split_qk(kc).transpose(0, 2, 3, 1, 4)
        v_lat, v_ctx = split_v(vx), split_v(vc)
        k_all = jnp.concatenate([k_lat, k_ctx], axis=3)
        v_all = jnp.concatenate([v_lat, v_ctx], axis=2)

        qb = q_lat.reshape(b, ATT_HEADS, 2, nblk, Q_BLOCK, QK_DIM).transpose(3, 0, 1, 2, 4, 5)
        att = lax.map(lambda qq: diff_attention(qq, k_all, v_all, lam, g_sub[l], lam_init), qb)
        att_lat = att.transpose(1, 0, 3, 2, 4).reshape(b, n, ATT_WIDTH)
        pool_lat = multiscale_pool(ux, w_pool[l], pool_scale[l])
        x = x + gt2 * (jnp.concatenate([att_lat, pool_lat], axis=-1) @ w_out[l])

        if not last:
            att_c = diff_attention(q_ctx, k_ctx, v_ctx, lam, g_sub[l], lam_init)
            att_c = att_c.transpose(0, 2, 1, 3).reshape(cx.shape[0], cx.shape[1], ATT_WIDTH)
            pool_c = multiscale_pool(uc, w_pool[l], pool_scale[l])
            cx = cx + cg2 * (jnp.concatenate([att_c, pool_c], axis=-1) @ w_out[l])
            cx = cx + 0.5 * cg3 * swiglu(modulate(rms_norm(cx, g_ffn2[l]), ch3, cc3), ffn2_w_in[l], ffn2_w_out[l])

        x = x + 0.5 * gt3 * swiglu(modulate(rms_norm(x, g_ffn2[l]), sh3, sc3), ffn2_w_in[l], ffn2_w_out[l])

    return rms_norm(x, g_final)
```

```python
import functools
import math

import jax
import jax.numpy as jnp
from jax import lax
from jax.experimental import pallas as pl
from jax.experimental.pallas import tpu as pltpu

D_MODEL = 1024
GRID_W = 64
ATT_WIDTH = 512
POOL_WIDTH = 512
ATT_HEADS = 4
HEAD_DIM = 128
QK_DIM = 64
POOL_WINDOWS = (2, 4, 8, 16)
POOL_GROUP_DIM = 128
D_FF = 2816
N_MOD = 9
ROPE_BASE = 10000.0
EPS = 1e-6
LAMBDA_INIT = 0.8 - 0.6 * math.exp(-0.3 * 0)

MOD_ROWS = 16
POOL_HALO = 8
NEG_BIG = -1e30
Q_SCALE = (QK_DIM ** -0.5) * math.log2(math.e)

VMEM_LIMIT = 56 * 1024 * 1024

TM_FFN = 512
TM_PROJ = 512
TM_MIX = 512
TQ_ATT = 512
TK_ATT = 512
FF_CHUNKS = (512, 512, 512, 512, 512, 256)

bf16 = jnp.bfloat16
f32 = jnp.float32


def _cparams(sem):
    return pltpu.CompilerParams(dimension_semantics=sem, vmem_limit_bytes=VMEM_LIMIT)


def _norm_mod(x, g, shift, scale):
    ms = jnp.mean(x * x, axis=-1, keepdims=True)
    y = x * lax.rsqrt(ms + EPS) * g
    return y * (1.0 + scale) + shift


def _mod_kernel(cond_ref, w_ref, b_ref, o_ref):
    cnd = cond_ref[...]
    a = (cnd * jax.nn.sigmoid(cnd)).astype(bf16)
    o_ref[...] = jnp.dot(a, w_ref[...].astype(bf16), preferred_element_type=f32) + b_ref[...]


def _mod_call(cond, w_mod, b_mod):
    n_out = w_mod.shape[1]
    tn = D_MODEL
    return pl.pallas_call(
        _mod_kernel,
        grid=(n_out // tn,),
        in_specs=[
            pl.BlockSpec((MOD_ROWS, D_MODEL), lambda j: (0, 0)),
            pl.BlockSpec((D_MODEL, tn), lambda j: (0, j)),
            pl.BlockSpec((1, tn), lambda j: (0, j)),
        ],
        out_specs=pl.BlockSpec((MOD_ROWS, tn), lambda j: (0, j)),
        out_shape=jax.ShapeDtypeStruct((MOD_ROWS, n_out), f32),
        compiler_params=_cparams(("arbitrary",)),
        name="mod",
    )(cond, w_mod, b_mod.reshape(1, n_out))


def _ffn_kernel(x_ref, mod_ref, g_ref, win_ref, wout_ref, *rest, k0, final):
    if final:
        gf_ref, o_ref = rest
    else:
        (o_ref,) = rest
    x = x_ref[0]
    shift = mod_ref[0, k0:k0 + 1, :]
    scale = mod_ref[0, k0 + 1:k0 + 2, :]
    gate = mod_ref[0, k0 + 2:k0 + 3, :]
    h = _norm_mod(x, g_ref[...], shift, scale).astype(bf16)
    acc = jnp.zeros(x.shape, f32)
    off = 0
    for tf in FF_CHUNKS:
        gt = jnp.dot(h, win_ref[:, off:off + tf], preferred_element_type=f32)
        up = jnp.dot(h, win_ref[:, D_FF + off:D_FF + off + tf], preferred_element_type=f32)
        a = (gt * jax.nn.sigmoid(gt) * up).astype(bf16)
        acc = acc + jnp.dot(a, wout_ref[off:off + tf, :], preferred_element_type=f32)
        off += tf
    y = x + 0.5 * gate * acc
    if final:
        ms = jnp.mean(y * y, axis=-1, keepdims=True)
        y = y * lax.rsqrt(ms + EPS) * gf_ref[...]
    o_ref[0] = y


def _ffn_call(x, mod, row_of_batch, k0, g, w_in, w_out, g_final=None):
    bx, t, d = x.shape
    tm = min(TM_FFN, t)
    final = g_final is not None
    in_specs = [
        pl.BlockSpec((1, tm, d), lambda b, i: (b, i, 0)),
        pl.BlockSpec((1, N_MOD, d), lambda b, i: (row_of_batch(b), 0, 0)),
        pl.BlockSpec((1, d), lambda b, i: (0, 0)),
        pl.BlockSpec(w_in.shape, lambda b, i: (0, 0), pipeline_mode=pl.Buffered(1)),
        pl.BlockSpec(w_out.shape, lambda b, i: (0, 0), pipeline_mode=pl.Buffered(1)),
    ]
    args = [x, mod, g.reshape(1, d), w_in, w_out]
    if final:
        in_specs.append(pl.BlockSpec((1, d), lambda b, i: (0, 0)))
        args.append(g_final.reshape(1, d))
    return pl.pallas_call(
        functools.partial(_ffn_kernel, k0=k0, final=final),
        grid=(bx, t // tm),
        in_specs=in_specs,
        out_specs=pl.BlockSpec((1, tm, d), lambda b, i: (b, i, 0)),
        out_shape=jax.ShapeDtypeStruct(x.shape, f32),
        compiler_params=_cparams(("parallel", "parallel")),
        name="ffn_final" if final else "ffn",
    )(*args)


def _rope(t, cos, sin_signed, first_half):
    w = t.shape[-1]
    partner = jnp.where(first_half, pltpu.roll(t, w - 16, axis=1), pltpu.roll(t, 16, axis=1))
    return t * cos + partner * sin_signed


def _inproj_kernel(x_ref, mod_ref, g_ref, w_ref, *rest, latent):
    if latent:
        cos_ref, sin_ref, q_ref, k_ref, v_ref, u_ref = rest
    else:
        k_ref, v_ref = rest
    x = x_ref[0]
    shift = mod_ref[0, 3:4, :]
    scale = mod_ref[0, 4:5, :]
    h = _norm_mod(x, g_ref[...], shift, scale).astype(bf16)
    hx = jnp.dot(h, w_ref[...], preferred_element_type=f32)
    tm = x.shape[0]
    lane = lax.broadcasted_iota(jnp.int32, (tm, HEAD_DIM), 1)
    map1 = lane < QK_DIM
    if latent:
        cos = jnp.concatenate([cos_ref[...]] * ATT_HEADS, axis=1)
        sin_signed = jnp.concatenate([sin_ref[...]] * ATT_HEADS, axis=1)
        lane_w = lax.broadcasted_iota(jnp.int32, (tm, ATT_WIDTH), 1)
        first_half = (lane_w % 32) < 16
        q = _rope(hx[:, :ATT_WIDTH], cos, sin_signed, first_half)
        q_ref[0] = (q * Q_SCALE).astype(bf16)
        k = _rope(hx[:, ATT_WIDTH:2 * ATT_WIDTH], cos, sin_signed, first_half)
        v = hx[:, 2 * ATT_WIDTH:3 * ATT_WIDTH]
        u_ref[0] = hx[:, 3 * ATT_WIDTH:]
    else:
        k = hx[:, :ATT_WIDTH]
        v = hx[:, ATT_WIDTH:]
    v_ref[0] = v.astype(bf16)
    for hh in range(ATT_HEADS):
        kh = k[:, hh * HEAD_DIM:(hh + 1) * HEAD_DIM]
        k_ref[0, hh, 0] = jnp.where(map1, kh, 0.0).astype(bf16)
        k_ref[0, hh, 1] = jnp.where(map1, 0.0, kh).astype(bf16)


def _inproj_call(x, mod, row_of_batch, g, w, rope_tabs=None):
    bx, t, d = x.shape
    tm = min(TM_PROJ, t)
    latent = rope_tabs is not None
    in_specs = [
        pl.BlockSpec((1, tm, d), lambda b, i: (b, i, 0)),
        pl.BlockSpec((1, N_MOD, d), lambda b, i: (row_of_batch(b), 0, 0)),
        pl.BlockSpec((1, d), lambda b, i: (0, 0)),
        pl.BlockSpec(w.shape, lambda b, i: (0, 0), pipeline_mode=pl.Buffered(1)),
    ]
    args = [x, mod, g.reshape(1, d), w]
    k_spec = pl.BlockSpec((1, ATT_HEADS, 2, tm, HEAD_DIM), lambda b, i: (b, 0, 0, i, 0))
    k_shape = jax.ShapeDtypeStruct((bx, ATT_HEADS, 2, t, HEAD_DIM), bf16)
    row_spec = pl.BlockSpec((1, tm, ATT_WIDTH), lambda b, i: (b, i, 0))
    if latent:
        in_specs += [pl.BlockSpec((tm, HEAD_DIM), lambda b, i: (i, 0))] * 2
        args += list(rope_tabs)
        out_specs = [row_spec, k_spec, row_spec, row_spec]
        out_shape = [jax.ShapeDtypeStruct((bx, t, ATT_WIDTH), bf16), k_shape,
                     jax.ShapeDtypeStruct((bx, t, ATT_WIDTH), bf16),
                     jax.ShapeDtypeStruct((bx, t, POOL_WIDTH), f32)]
    else:
        out_specs = [k_spec, row_spec]
        out_shape = [k_shape, jax.ShapeDtypeStruct((bx, t, ATT_WIDTH), bf16)]
    return pl.pallas_call(
        functools.partial(_inproj_kernel, latent=latent),
        grid=(bx, t // tm),
        in_specs=in_specs,
        out_specs=out_specs,
        out_shape=out_shape,
        compiler_params=_cparams(("parallel", "parallel")),
        name="inproj_lat" if latent else "inproj_ctx",
    )(*args)


def _attn_kernel(lam_ref, gsub_ref, q_ref, kl_ref, kc_ref, vl_ref, vc_ref, o_ref):
    q = q_ref[0]
    tq = q.shape[0]
    n_lat = vl_ref.shape[1]

    def update(carry, k_of_map, v):
        new = []
        for mi in range(2):
            m, l, acc = carry[mi]
            s = lax.dot_general(q, k_of_map(mi), (((1,), (1,)), ((), ())),
                                preferred_element_type=f32)
            m_new = jnp.maximum(m, jnp.max(s, axis=-1, keepdims=True))
            alpha = jnp.exp2(m - m_new)
            p = jnp.exp2(s - m_new)
            l = alpha * l + jnp.sum(p, axis=-1, keepdims=True)
            acc = alpha * acc + jnp.dot(p.astype(bf16), v, preferred_element_type=f32)
            new.append((m_new, l, acc))
        return tuple(new)

    def body(j, carry):
        start = pl.multiple_of(j * TK_ATT, TK_ATT)
        return update(carry,
                      lambda mi: kl_ref[0, 0, mi, pl.ds(start, TK_ATT), :],
                      vl_ref[0, pl.ds(start, TK_ATT), :])

    init = tuple((jnp.full((tq, 1), NEG_BIG, f32), jnp.zeros((tq, 1), f32),
                  jnp.zeros((tq, HEAD_DIM), f32)) for _ in range(2))
    carry = lax.fori_loop(0, n_lat // TK_ATT, body, init)
    carry = update(carry, lambda mi: kc_ref[0, 0, mi], vc_ref[0])

    lam_v = lam_ref[...]
    lam = (jnp.exp(jnp.sum(lam_v[0:1] * lam_v[1:2], axis=-1, keepdims=True))
           - jnp.exp(jnp.sum(lam_v[2:3] * lam_v[3:4], axis=-1, keepdims=True))
           + LAMBDA_INIT)
    (_, l1, a1), (_, l2, a2) = carry
    o = a1 / l1 - lam * (a2 / l2)
    ms = jnp.mean(o * o, axis=-1, keepdims=True)
    o = o * lax.rsqrt(ms + EPS) * gsub_ref[...] * (1.0 - LAMBDA_INIT)
    o_ref[0] = o.astype(bf16)


def _attn_call(lam_rows, g_sub, q, k_lat, k_ctx, v_lat, v_ctx):
    b, n, _ = q.shape
    n_ctx = v_ctx.shape[1]
    tq = TQ_ATT
    return pl.pallas_call(
        _attn_kernel,
        grid=(b, ATT_HEADS, n // tq),
        in_specs=[
            pl.BlockSpec((4, QK_DIM), lambda bb, h, i: (0, 0)),
            pl.BlockSpec((1, HEAD_DIM), lambda bb, h, i: (0, 0)),
            pl.BlockSpec((1, tq, HEAD_DIM), lambda bb, h, i: (bb, i, h)),
            pl.BlockSpec((1, 1, 2, n, HEAD_DIM), lambda bb, h, i: (bb, h, 0, 0, 0)),
            pl.BlockSpec((1, 1, 2, n_ctx, HEAD_DIM), lambda bb, h, i: (bb, h, 0, 0, 0)),
            pl.BlockSpec((1, n, HEAD_DIM), lambda bb, h, i: (bb, 0, h)),
            pl.BlockSpec((1, n_ctx, HEAD_DIM), lambda bb, h, i: (bb, 0, h)),
        ],
        out_specs=pl.BlockSpec((1, tq, HEAD_DIM), lambda bb, h, i: (bb, i, h)),
        out_shape=jax.ShapeDtypeStruct((b, n, ATT_WIDTH), bf16),
        compiler_params=_cparams(("parallel", "parallel", "arbitrary")),
        name="attn",
    )(lam_rows, g_sub.reshape(1, HEAD_DIM), q, k_lat, k_ctx, v_lat, v_ctx)


def _mixout_kernel(x_ref, mod_ref, att_ref, u_ref, up_ref, un_ref, wp_ref, ps_ref, wo_ref,
                   o_ref, ubuf, *, n_tok):
    i = pl.program_id(1)
    nt = pl.num_programs(1)
    tm = x_ref.shape[1]
    hl = POOL_HALO
    ubuf[0:hl, :] = jnp.where(i > 0, up_ref[0], 0.0)
    ubuf[hl:hl + tm, :] = u_ref[0]
    ubuf[hl + tm:, :] = jnp.where(i < nt - 1, un_ref[0], 0.0)

    t = i * tm + lax.broadcasted_iota(jnp.int32, (tm, 1), 0)
    pooled = []
    for gi, w in enumerate(POOL_WINDOWS):
        lo = w // 2
        hi = w - lo - 1
        sl = slice(gi * POOL_GROUP_DIM, (gi + 1) * POOL_GROUP_DIM)
        total = ubuf[hl - lo:hl - lo + tm, sl]
        for d in range(-lo + 1, hi + 1):
            total = total + ubuf[hl + d:hl + d + tm, sl]
        cnt = (jnp.minimum(t + hi, n_tok - 1) - jnp.maximum(t - lo, 0) + 1).astype(f32)
        diff = (total / cnt - ubuf[hl:hl + tm, sl]).astype(bf16)
        pooled.append(jnp.dot(diff, wp_ref[gi], preferred_element_type=f32))
    pool = (jnp.concatenate(pooled, axis=-1) * ps_ref[...]).astype(bf16)
    y = jnp.dot(att_ref[0], wo_ref[0:ATT_WIDTH, :], preferred_element_type=f32)
    y = y + jnp.dot(pool, wo_ref[ATT_WIDTH:, :], preferred_element_type=f32)
    o_ref[0] = x_ref[0] + mod_ref[0, 5:6, :] * y


def _mixout_call(x, mod, att, u, w_pool, pool_scale, w_out):
    b, n, d = x.shape
    tm = TM_MIX
    hb = tm // POOL_HALO
    last_halo_block = n // POOL_HALO - 1
    return pl.pallas_call(
        functools.partial(_mixout_kernel, n_tok=n),
        grid=(b, n // tm),
        in_specs=[
            pl.BlockSpec((1, tm, d), lambda bb, i: (bb, i, 0)),
            pl.BlockSpec((1, N_MOD, d), lambda bb, i: (bb, 0, 0)),
            pl.BlockSpec((1, tm, ATT_WIDTH), lambda bb, i: (bb, i, 0)),
            pl.BlockSpec((1, tm, POOL_WIDTH), lambda bb, i: (bb, i, 0)),
            pl.BlockSpec((1, POOL_HALO, POOL_WIDTH),
                         lambda bb, i: (bb, jnp.maximum(i * hb - 1, 0), 0)),
            pl.BlockSpec((1, POOL_HALO, POOL_WIDTH),
                         lambda bb, i: (bb, jnp.minimum((i + 1) * hb, last_halo_block), 0)),
            pl.BlockSpec(w_pool.shape, lambda bb, i: (0, 0, 0)),
            pl.BlockSpec((1, POOL_WIDTH), lambda bb, i: (0, 0)),
            pl.BlockSpec(w_out.shape, lambda bb, i: (0, 0)),
        ],
        out_specs=pl.BlockSpec((1, tm, d), lambda bb, i: (bb, i, 0)),
        out_shape=jax.ShapeDtypeStruct(x.shape, f32),
        scratch_shapes=[pltpu.VMEM((tm + 2 * POOL_HALO, POOL_WIDTH), f32)],
        compiler_params=_cparams(("parallel", "parallel")),
        name="mixout",
    )(x, mod, att, u, u, u, w_pool, pool_scale.reshape(1, POOL_WIDTH), w_out)


def _rope_tables(n):
    rows = n // GRID_W
    row = jnp.repeat(jnp.arange(rows, dtype=f32), GRID_W)
    col = jnp.tile(jnp.arange(GRID_W, dtype=f32), rows)
    nf = QK_DIM // 4
    freqs = ROPE_BASE ** (-jnp.arange(nf, dtype=f32) / nf)
    ar = row[:, None] * freqs
    ac = col[:, None] * freqs
    ang = jnp.concatenate([ar, ar, ac, ac], axis=-1)
    sign = jnp.tile(jnp.concatenate([-jnp.ones((nf,), f32), jnp.ones((nf,), f32)]), 2)
    cos = jnp.tile(jnp.cos(ang), (1, 2))
    sin_signed = jnp.tile(jnp.sin(ang) * sign, (1, 2))
    return cos, sin_signed


def kernel(x, c, ctx, c_ctx, w_mod, b_mod, g_ffn1, ffn1_w_in, ffn1_w_out, g_mix, w_in,
           lambda_q1, lambda_k1, lambda_q2, lambda_k2, g_sub, w_pool, pool_scale, w_out,
           g_ffn2, ffn2_w_in, ffn2_w_out, g_final):
    b, n, d = x.shape
    l = 0
    cond = jnp.zeros((MOD_ROWS, d), f32).at[:b].set(c).at[b].set(c_ctx)
    mod = _mod_call(cond, w_mod[l], b_mod[l]).reshape(MOD_ROWS, N_MOD, d)
    lat_row = lambda bb: bb
    ctx_row = lambda bb: b

    w1_in, w1_out = ffn1_w_in[l].astype(bf16), ffn1_w_out[l].astype(bf16)
    w2_in, w2_out = ffn2_w_in[l].astype(bf16), ffn2_w_out[l].astype(bf16)
    w_in_b = w_in[l].astype(bf16)
    w_o_b = w_out[l].astype(bf16)
    w_pool_b = w_pool[l].astype(bf16)

    x1 = _ffn_call(x, mod, lat_row, 0, g_ffn1[l], w1_in, w1_out)
    cx1 = _ffn_call(ctx, mod, ctx_row, 0, g_ffn1[l], w1_in, w1_out)

    q, k_lat, v_lat, u = _inproj_call(x1, mod, lat_row, g_mix[l], w_in_b, _rope_tables(n))
    k_ctx, v_ctx = _inproj_call(cx1, mod, ctx_row, g_mix[l],
                                w_in_b[:, ATT_WIDTH:3 * ATT_WIDTH])

    lam_rows = jnp.stack([lambda_q1[l], lambda_k1[l], lambda_q2[l], lambda_k2[l]]).astype(f32)
    att = _attn_call(lam_rows, g_sub[l], q, k_lat, k_ctx, v_lat, v_ctx)

    x2 = _mixout_call(x1, mod, att, u, w_pool_b, pool_scale[l], w_o_b)
    return _ffn_call(x2, mod, lat_row, 6, g_ffn2[l], w2_in, w2_out, g_final)
```

```python
import functools
import math

import jax
import jax.numpy as jnp
from jax import lax
from jax.experimental import pallas as pl
from jax.experimental.pallas import tpu as pltpu

D_MODEL = 1024
GRID_W = 64
ATT_WIDTH = 512
POOL_WIDTH = 512
ATT_HEADS = 4
HEAD_DIM = 128
QK_DIM = 64
POOL_WINDOWS = (2, 4, 8, 16)
POOL_GROUP_DIM = 128
D_FF = 2816
N_MOD = 9
ROPE_BASE = 10000.0
EPS = 1e-6
LAMBDA_INIT = 0.8 - 0.6 * math.exp(-0.3 * 0)

MOD_ROWS = 16
POOL_HALO = 8
NEG_BIG = -1e30
Q_SCALE = (QK_DIM ** -0.5) * math.log2(math.e)

VMEM_LIMIT = 56 * 1024 * 1024

TM_FFN = 512
TM_PROJ = 512
TM_MIX = 512
TQ_ATT = 512
TK_ATT = 1024
RB_ATT = 32
FF_CHUNKS = (512, 512, 512, 512, 512, 256)

bf16 = jnp.bfloat16
f32 = jnp.float32


def _cparams(sem):
    return pltpu.CompilerParams(dimension_semantics=sem, vmem_limit_bytes=VMEM_LIMIT)


def _norm_mod(x, g, shift, scale):
    ms = jnp.mean(x * x, axis=-1, keepdims=True)
    y = x * lax.rsqrt(ms + EPS) * g
    return y * (1.0 + scale) + shift


def _mod_kernel(cond_ref, w_ref, b_ref, o_ref):
    cnd = cond_ref[...]
    a = (cnd * jax.nn.sigmoid(cnd)).astype(bf16)
    o_ref[...] = jnp.dot(a, w_ref[...].astype(bf16), preferred_element_type=f32) + b_ref[...]


def _mod_call(cond, w_mod, b_mod):
    n_out = w_mod.shape[1]
    tn = D_MODEL
    return pl.pallas_call(
        _mod_kernel,
        grid=(n_out // tn,),
        in_specs=[
            pl.BlockSpec((MOD_ROWS, D_MODEL), lambda j: (0, 0)),
            pl.BlockSpec((D_MODEL, tn), lambda j: (0, j)),
            pl.BlockSpec((1, tn), lambda j: (0, j)),
        ],
        out_specs=pl.BlockSpec((MOD_ROWS, tn), lambda j: (0, j)),
        out_shape=jax.ShapeDtypeStruct((MOD_ROWS, n_out), f32),
        compiler_params=_cparams(("arbitrary",)),
        name="mod",
    )(cond, w_mod, b_mod.reshape(1, n_out))


def _ffn_kernel(x_ref, mod_ref, g_ref, win_ref, wout_ref, *rest, k0, final):
    if final:
        gf_ref, o_ref = rest
    else:
        (o_ref,) = rest
    x = x_ref[0]
    shift = mod_ref[0, k0:k0 + 1, :]
    scale = mod_ref[0, k0 + 1:k0 + 2, :]
    gate = mod_ref[0, k0 + 2:k0 + 3, :]
    h = _norm_mod(x, g_ref[...], shift, scale).astype(bf16)
    acc = jnp.zeros(x.shape, f32)
    off = 0
    for tf in FF_CHUNKS:
        gt = jnp.dot(h, win_ref[:, off:off + tf], preferred_element_type=f32)
        up = jnp.dot(h, win_ref[:, D_FF + off:D_FF + off + tf], preferred_element_type=f32)
        a = (gt * jax.nn.sigmoid(gt) * up).astype(bf16)
        acc = acc + jnp.dot(a, wout_ref[off:off + tf, :], preferred_element_type=f32)
        off += tf
    y = x + 0.5 * gate * acc
    if final:
        ms = jnp.mean(y * y, axis=-1, keepdims=True)
        y = y * lax.rsqrt(ms + EPS) * gf_ref[...]
    o_ref[0] = y


def _ffn_call(x, mod, row_of_batch, k0, g, w_in, w_out, g_final=None):
    bx, t, d = x.shape
    tm = min(TM_FFN, t)
    final = g_final is not None
    in_specs = [
        pl.BlockSpec((1, tm, d), lambda b, i: (b, i, 0)),
        pl.BlockSpec((1, N_MOD, d), lambda b, i: (row_of_batch(b), 0, 0)),
        pl.BlockSpec((1, d), lambda b, i: (0, 0)),
        pl.BlockSpec(w_in.shape, lambda b, i: (0, 0), pipeline_mode=pl.Buffered(1)),
        pl.BlockSpec(w_out.shape, lambda b, i: (0, 0), pipeline_mode=pl.Buffered(1)),
    ]
    args = [x, mod, g.reshape(1, d), w_in, w_out]
    if final:
        in_specs.append(pl.BlockSpec((1, d), lambda b, i: (0, 0)))
        args.append(g_final.reshape(1, d))
    return pl.pallas_call(
        functools.partial(_ffn_kernel, k0=k0, final=final),
        grid=(bx, t // tm),
        in_specs=in_specs,
        out_specs=pl.BlockSpec((1, tm, d), lambda b, i: (b, i, 0)),
        out_shape=jax.ShapeDtypeStruct(x.shape, f32),
        compiler_params=_cparams(("parallel", "parallel")),
        name="ffn_final" if final else "ffn",
    )(*args)


def _rope(t, cos, sin_signed, first_half):
    w = t.shape[-1]
    partner = jnp.where(first_half, pltpu.roll(t, w - 16, axis=1), pltpu.roll(t, 16, axis=1))
    return t * cos + partner * sin_signed


def _inproj_kernel(x_ref, mod_ref, g_ref, w_ref, *rest, latent):
    if latent:
        cos_ref, sin_ref, q_ref, k_ref, v_ref, u_ref = rest
    else:
        k_ref, v_ref = rest
    x = x_ref[0]
    shift = mod_ref[0, 3:4, :]
    scale = mod_ref[0, 4:5, :]
    h = _norm_mod(x, g_ref[...], shift, scale).astype(bf16)
    hx = jnp.dot(h, w_ref[...], preferred_element_type=f32)
    tm = x.shape[0]
    lane = lax.broadcasted_iota(jnp.int32, (tm, HEAD_DIM), 1)
    map1 = lane < QK_DIM
    if latent:
        cos = jnp.concatenate([cos_ref[...]] * ATT_HEADS, axis=1)
        sin_signed = jnp.concatenate([sin_ref[...]] * ATT_HEADS, axis=1)
        lane_w = lax.broadcasted_iota(jnp.int32, (tm, ATT_WIDTH), 1)
        first_half = (lane_w % 32) < 16
        q = _rope(hx[:, :ATT_WIDTH], cos, sin_signed, first_half)
        q_ref[0] = (q * Q_SCALE).astype(bf16)
        k = _rope(hx[:, ATT_WIDTH:2 * ATT_WIDTH], cos, sin_signed, first_half)
        v = hx[:, 2 * ATT_WIDTH:3 * ATT_WIDTH]
        u_ref[0] = hx[:, 3 * ATT_WIDTH:]
    else:
        k = hx[:, :ATT_WIDTH]
        v = hx[:, ATT_WIDTH:]
    v_ref[0] = v.astype(bf16)
    for hh in range(ATT_HEADS):
        kh = k[:, hh * HEAD_DIM:(hh + 1) * HEAD_DIM]
        k_ref[0, hh, 0] = jnp.where(map1, kh, 0.0).astype(bf16)
        k_ref[0, hh, 1] = jnp.where(map1, 0.0, kh).astype(bf16)


def _inproj_call(x, mod, row_of_batch, g, w, rope_tabs=None):
    bx, t, d = x.shape
    tm = min(TM_PROJ, t)
    latent = rope_tabs is not None
    in_specs = [
        pl.BlockSpec((1, tm, d), lambda b, i: (b, i, 0)),
        pl.BlockSpec((1, N_MOD, d), lambda b, i: (row_of_batch(b), 0, 0)),
        pl.BlockSpec((1, d), lambda b, i: (0, 0)),
        pl.BlockSpec(w.shape, lambda b, i: (0, 0), pipeline_mode=pl.Buffered(1)),
    ]
    args = [x, mod, g.reshape(1, d), w]
    k_spec = pl.BlockSpec((1, ATT_HEADS, 2, tm, HEAD_DIM), lambda b, i: (b, 0, 0, i, 0))
    k_shape = jax.ShapeDtypeStruct((bx, ATT_HEADS, 2, t, HEAD_DIM), bf16)
    row_spec = pl.BlockSpec((1, tm, ATT_WIDTH), lambda b, i: (b, i, 0))
    if latent:
        in_specs += [pl.BlockSpec((tm, HEAD_DIM), lambda b, i: (i, 0))] * 2
        args += list(rope_tabs)
        out_specs = [row_spec, k_spec, row_spec, row_spec]
        out_shape = [jax.ShapeDtypeStruct((bx, t, ATT_WIDTH), bf16), k_shape,
                     jax.ShapeDtypeStruct((bx, t, ATT_WIDTH), bf16),
                     jax.ShapeDtypeStruct((bx, t, POOL_WIDTH), f32)]
    else:
        out_specs = [k_spec, row_spec]
        out_shape = [k_shape, jax.ShapeDtypeStruct((bx, t, ATT_WIDTH), bf16)]
    return pl.pallas_call(
        functools.partial(_inproj_kernel, latent=latent),
        grid=(bx, t // tm),
        in_specs=in_specs,
        out_specs=out_specs,
        out_shape=out_shape,
        compiler_params=_cparams(("parallel", "parallel")),
        name="inproj_lat" if latent else "inproj_ctx",
    )(*args)


def _attn_kernel(lam_ref, gsub_ref, q_ref, kl_ref, kc_ref, vl_ref, vc_ref, o_ref,
                 s_scr, p_scr, m_scr, l_scr, acc_scr):
    q = q_ref[0]
    tq = q.shape[0]
    n_lat = vl_ref.shape[1]
    m_scr[...] = jnp.full(m_scr.shape, NEG_BIG, f32)
    l_scr[...] = jnp.zeros(l_scr.shape, f32)
    acc_scr[...] = jnp.zeros(acc_scr.shape, f32)

    def chunk(k_of_map, v, tk):
        for mi in range(2):
            s_scr[mi, :, :tk] = lax.dot_general(q, k_of_map(mi), (((1,), (1,)), ((), ())),
                                                preferred_element_type=f32)
        for mi in range(2):
            for r in range(tq // RB_ATT):
                rs = slice(r * RB_ATT, (r + 1) * RB_ATT)
                s = s_scr[mi, rs, :tk]
                m_old = m_scr[mi, rs, :]
                m_new = jnp.maximum(m_old, jnp.max(s, axis=-1, keepdims=True))
                alpha = jnp.exp2(m_old - m_new)
                p = jnp.exp2(s - m_new)
                psum = p[:, 0:HEAD_DIM]
                for c in range(1, tk // HEAD_DIM):
                    psum = psum + p[:, c * HEAD_DIM:(c + 1) * HEAD_DIM]
                l_scr[mi, rs, :] = alpha * l_scr[mi, rs, :] + psum
                acc_scr[mi, rs, :] = alpha * acc_scr[mi, rs, :]
                m_scr[mi, rs, :] = m_new
                p_scr[mi, rs, :tk] = p.astype(bf16)
            acc_scr[mi] = acc_scr[mi] + jnp.dot(p_scr[mi, :, :tk], v,
                                                preferred_element_type=f32)

    def body(j, carry):
        start = pl.multiple_of(j * TK_ATT, TK_ATT)
        chunk(lambda mi: kl_ref[0, 0, mi, pl.ds(start, TK_ATT), :],
              vl_ref[0, pl.ds(start, TK_ATT), :], TK_ATT)
        return carry

    lax.fori_loop(0, n_lat // TK_ATT, body, 0)
    chunk(lambda mi: kc_ref[0, 0, mi], vc_ref[0], vc_ref.shape[1])

    lam_v = lam_ref[...]
    lam = (jnp.exp(jnp.sum(lam_v[0:1] * lam_v[1:2], axis=-1, keepdims=True))
           - jnp.exp(jnp.sum(lam_v[2:3] * lam_v[3:4], axis=-1, keepdims=True))
           + LAMBDA_INIT)
    l1 = jnp.sum(l_scr[0], axis=-1, keepdims=True)
    l2 = jnp.sum(l_scr[1], axis=-1, keepdims=True)
    o = acc_scr[0] / l1 - lam * (acc_scr[1] / l2)
    ms = jnp.mean(o * o, axis=-1, keepdims=True)
    o = o * lax.rsqrt(ms + EPS) * gsub_ref[...] * (1.0 - LAMBDA_INIT)
    o_ref[0] = o.astype(bf16)


def _attn_call(lam_rows, g_sub, q, k_lat, k_ctx, v_lat, v_ctx):
    b, n, _ = q.shape
    n_ctx = v_ctx.shape[1]
    tq = TQ_ATT
    return pl.pallas_call(
        _attn_kernel,
        grid=(b, ATT_HEADS, n // tq),
        in_specs=[
            pl.BlockSpec((4, QK_DIM), lambda bb, h, i: (0, 0)),
            pl.BlockSpec((1, HEAD_DIM), lambda bb, h, i: (0, 0)),
            pl.BlockSpec((1, tq, HEAD_DIM), lambda bb, h, i: (bb, i, h)),
            pl.BlockSpec((1, 1, 2, n, HEAD_DIM), lambda bb, h, i: (bb, h, 0, 0, 0)),
            pl.BlockSpec((1, 1, 2, n_ctx, HEAD_DIM), lambda bb, h, i: (bb, h, 0, 0, 0)),
            pl.BlockSpec((1, n, HEAD_DIM), lambda bb, h, i: (bb, 0, h)),
            pl.BlockSpec((1, n_ctx, HEAD_DIM), lambda bb, h, i: (bb, 0, h)),
        ],
        out_specs=pl.BlockSpec((1, tq, HEAD_DIM), lambda bb, h, i: (bb, i, h)),
        out_shape=jax.ShapeDtypeStruct((b, n, ATT_WIDTH), bf16),
        scratch_shapes=[
            pltpu.VMEM((2, tq, TK_ATT), f32),
            pltpu.VMEM((2, tq, TK_ATT), bf16),
            pltpu.VMEM((2, tq, 1), f32),
            pltpu.VMEM((2, tq, HEAD_DIM), f32),
            pltpu.VMEM((2, tq, HEAD_DIM), f32),
        ],
        compiler_params=_cparams(("parallel", "parallel", "arbitrary")),
        name="attn",
    )(lam_rows, g_sub.reshape(1, HEAD_DIM), q, k_lat, k_ctx, v_lat, v_ctx)


def _mixout_kernel(x_ref, mod_ref, att_ref, u_ref, up_ref, un_ref, wp_ref, ps_ref, wo_ref,
                   o_ref, ubuf, *, n_tok):
    i = pl.program_id(1)
    nt = pl.num_programs(1)
    tm = x_ref.shape[1]
    hl = POOL_HALO
    ubuf[0:hl, :] = jnp.where(i > 0, up_ref[0], 0.0)
    ubuf[hl:hl + tm, :] = u_ref[0]
    ubuf[hl + tm:, :] = jnp.where(i < nt - 1, un_ref[0], 0.0)

    t = i * tm + lax.broadcasted_iota(jnp.int32, (tm, 1), 0)
    pooled = []
    for gi, w in enumerate(POOL_WINDOWS):
        lo = w // 2
        hi = w - lo - 1
        sl = slice(gi * POOL_GROUP_DIM, (gi + 1) * POOL_GROUP_DIM)
        total = ubuf[hl - lo:hl - lo + tm, sl]
        for d in range(-lo + 1, hi + 1):
            total = total + ubuf[hl + d:hl + d + tm, sl]
        cnt = (jnp.minimum(t + hi, n_tok - 1) - jnp.maximum(t - lo, 0) + 1).astype(f32)
        diff = (total / cnt - ubuf[hl:hl + tm, sl]).astype(bf16)
        pooled.append(jnp.dot(diff, wp_ref[gi], preferred_element_type=f32))
    pool = (jnp.concatenate(pooled, axis=-1) * ps_ref[...]).astype(bf16)
    y = jnp.dot(att_ref[0], wo_ref[0:ATT_WIDTH, :], preferred_element_type=f32)
    y = y + jnp.dot(pool, wo_ref[ATT_WIDTH:, :], preferred_element_type=f32)
    o_ref[0] = x_ref[0] + mod_ref[0, 5:6, :] * y


def _mixout_call(x, mod, att, u, w_pool, pool_scale, w_out):
    b, n, d = x.shape
    tm = TM_MIX
    hb = tm // POOL_HALO
    last_halo_block = n // POOL_HALO - 1
    return pl.pallas_call(
        functools.partial(_mixout_kernel, n_tok=n),
        grid=(b, n // tm),
        in_specs=[
            pl.BlockSpec((1, tm, d), lambda bb, i: (bb, i, 0)),
            pl.BlockSpec((1, N_MOD, d), lambda bb, i: (bb, 0, 0)),
            pl.BlockSpec((1, tm, ATT_WIDTH), lambda bb, i: (bb, i, 0)),
            pl.BlockSpec((1, tm, POOL_WIDTH), lambda bb, i: (bb, i, 0)),
            pl.BlockSpec((1, POOL_HALO, POOL_WIDTH),
                         lambda bb, i: (bb, jnp.maximum(i * hb - 1, 0), 0)),
            pl.BlockSpec((1, POOL_HALO, POOL_WIDTH),
                         lambda bb, i: (bb, jnp.minimum((i + 1) * hb, last_halo_block), 0)),
            pl.BlockSpec(w_pool.shape, lambda bb, i: (0, 0, 0)),
            pl.BlockSpec((1, POOL_WIDTH), lambda bb, i: (0, 0)),
            pl.BlockSpec(w_out.shape, lambda bb, i: (0, 0)),
        ],
        out_specs=pl.BlockSpec((1, tm, d), lambda bb, i: (bb, i, 0)),
        out_shape=jax.ShapeDtypeStruct(x.shape, f32),
        scratch_shapes=[pltpu.VMEM((tm + 2 * POOL_HALO, POOL_WIDTH), f32)],
        compiler_params=_cparams(("parallel", "parallel")),
        name="mixout",
    )(x, mod, att, u, u, u, w_pool, pool_scale.reshape(1, POOL_WIDTH), w_out)


def _rope_tables(n):
    rows = n // GRID_W
    row = jnp.repeat(jnp.arange(rows, dtype=f32), GRID_W)
    col = jnp.tile(jnp.arange(GRID_W, dtype=f32), rows)
    nf = QK_DIM // 4
    freqs = ROPE_BASE ** (-jnp.arange(nf, dtype=f32) / nf)
    ar = row[:, None] * freqs
    ac = col[:, None] * freqs
    ang = jnp.concatenate([ar, ar, ac, ac], axis=-1)
    sign = jnp.tile(jnp.concatenate([-jnp.ones((nf,), f32), jnp.ones((nf,), f32)]), 2)
    cos = jnp.tile(jnp.cos(ang), (1, 2))
    sin_signed = jnp.tile(jnp.sin(ang) * sign, (1, 2))
    return cos, sin_signed


def kernel(x, c, ctx, c_ctx, w_mod, b_mod, g_ffn1, ffn1_w_in, ffn1_w_out, g_mix, w_in,
           lambda_q1, lambda_k1, lambda_q2, lambda_k2, g_sub, w_pool, pool_scale, w_out,
           g_ffn2, ffn2_w_in, ffn2_w_out, g_final):
    b, n, d = x.shape
    l = 0
    cond = jnp.zeros((MOD_ROWS, d), f32).at[:b].set(c).at[b].set(c_ctx)
    mod = _mod_call(cond, w_mod[l], b_mod[l]).reshape(MOD_ROWS, N_MOD, d)
    lat_row = lambda bb: bb
    ctx_row = lambda bb: b

    w1_in, w1_out = ffn1_w_in[l].astype(bf16), ffn1_w_out[l].astype(bf16)
    w2_in, w2_out = ffn2_w_in[l].astype(bf16), ffn2_w_out[l].astype(bf16)
    w_in_b = w_in[l].astype(bf16)
    w_o_b = w_out[l].astype(bf16)
    w_pool_b = w_pool[l].astype(bf16)

    x1 = _ffn_call(x, mod, lat_row, 0, g_ffn1[l], w1_in, w1_out)
    cx1 = _ffn_call(ctx, mod, ctx_row, 0, g_ffn1[l], w1_in, w1_out)

    q, k_lat, v_lat, u = _inproj_call(x1, mod, lat_row, g_mix[l], w_in_b, _rope_tables(n))
    k_ctx, v_ctx = _inproj_call(cx1, mod, ctx_row, g_mix[l],
                                w_in_b[:, ATT_WIDTH:3 * ATT_WIDTH])

    lam_rows = jnp.stack([lambda_q1[l], lambda_k1[l], lambda_q2[l], lambda_k2[l]]).astype(f32)
    att = _attn_call(lam_rows, g_sub[l], q, k_lat, k_ctx, v_lat, v_ctx)

    x2 = _mixout_call(x1, mod, att, u, w_pool_b, pool_scale[l], w_o_b)
    return _ffn_call(x2, mod, lat_row, 6, g_ffn2[l], w2_in, w2_out, g_final)
```

```python
import functools
import math

import jax
import jax.numpy as jnp
from jax import lax
from jax.experimental import pallas as pl
from jax.experimental.pallas import tpu as pltpu

D_MODEL = 1024
GRID_W = 64
ATT_WIDTH = 512
POOL_WIDTH = 512
ATT_HEADS = 4
HEAD_DIM = 128
QK_DIM = 64
POOL_WINDOWS = (2, 4, 8, 16)
POOL_GROUP_DIM = 128
D_FF = 2816
N_MOD = 9
ROPE_BASE = 10000.0
EPS = 1e-6
LAMBDA_INIT = 0.8 - 0.6 * math.exp(-0.3 * 0)

MXU_COLS = 256
MOD_ROWS = 16
POOL_HALO = 8
NEG_BIG = -1e30
Q_SCALE = (QK_DIM ** -0.5) * math.log2(math.e)

VMEM_LIMIT = 56 * 1024 * 1024

TM_FFN = 512
TM_PROJ = 512
TM_MIX = 512
TQ_ATT = 512
TK_ATT = 1024
RB_ATT = 64
VT_ROWS = HEAD_DIM + 16
FF_CHUNKS = (512, 512, 512, 512, 512, 256)

bf16 = jnp.bfloat16
f32 = jnp.float32


def _cparams(sem):
    return pltpu.CompilerParams(dimension_semantics=sem, vmem_limit_bytes=VMEM_LIMIT)


def _norm_mod(x, g, shift, scale):
    ms = jnp.mean(x * x, axis=-1, keepdims=True)
    y = x * lax.rsqrt(ms + EPS) * g
    return y * (1.0 + scale) + shift


def _mod_kernel(cond_ref, w_ref, b_ref, o_ref):
    cnd = cond_ref[...]
    a = (cnd * jax.nn.sigmoid(cnd)).astype(bf16)
    o_ref[...] = jnp.dot(a, w_ref[...].astype(bf16), preferred_element_type=f32) + b_ref[...]


def _mod_call(cond, w_mod, b_mod):
    n_out = w_mod.shape[1]
    tn = D_MODEL
    return pl.pallas_call(
        _mod_kernel,
        grid=(n_out // tn,),
        in_specs=[
            pl.BlockSpec((MOD_ROWS, D_MODEL), lambda j: (0, 0)),
            pl.BlockSpec((D_MODEL, tn), lambda j: (0, j)),
            pl.BlockSpec((1, tn), lambda j: (0, j)),
        ],
        out_specs=pl.BlockSpec((MOD_ROWS, tn), lambda j: (0, j)),
        out_shape=jax.ShapeDtypeStruct((MOD_ROWS, n_out), f32),
        compiler_params=_cparams(("arbitrary",)),
        name="mod",
    )(cond, w_mod, b_mod.reshape(1, n_out))


def _ffn_kernel(x_ref, mod_ref, g_ref, win_ref, wout_ref, *rest, k0, final):
    if final:
        gf_ref, o_ref = rest
    else:
        (o_ref,) = rest
    x = x_ref[0]
    shift = mod_ref[0, k0:k0 + 1, :]
    scale = mod_ref[0, k0 + 1:k0 + 2, :]
    gate = mod_ref[0, k0 + 2:k0 + 3, :]
    h = _norm_mod(x, g_ref[...], shift, scale).astype(bf16)
    acc = jnp.zeros(x.shape, f32)
    off = 0
    for tf in FF_CHUNKS:
        gt = jnp.dot(h, win_ref[:, off:off + tf], preferred_element_type=f32)
        up = jnp.dot(h, win_ref[:, D_FF + off:D_FF + off + tf], preferred_element_type=f32)
        a = (gt * jax.nn.sigmoid(gt) * up).astype(bf16)
        acc = acc + jnp.dot(a, wout_ref[off:off + tf, :], preferred_element_type=f32)
        off += tf
    y = x + 0.5 * gate * acc
    if final:
        ms = jnp.mean(y * y, axis=-1, keepdims=True)
        y = y * lax.rsqrt(ms + EPS) * gf_ref[...]
    o_ref[0] = y


def _ffn_call(x, mod, row_of_batch, k0, g, w_in, w_out, g_final=None):
    bx, t, d = x.shape
    tm = min(TM_FFN, t)
    final = g_final is not None
    in_specs = [
        pl.BlockSpec((1, tm, d), lambda b, i: (b, i, 0)),
        pl.BlockSpec((1, N_MOD, d), lambda b, i: (row_of_batch(b), 0, 0)),
        pl.BlockSpec((1, d), lambda b, i: (0, 0)),
        pl.BlockSpec(w_in.shape, lambda b, i: (0, 0), pipeline_mode=pl.Buffered(1)),
        pl.BlockSpec(w_out.shape, lambda b, i: (0, 0), pipeline_mode=pl.Buffered(1)),
    ]
    args = [x, mod, g.reshape(1, d), w_in, w_out]
    if final:
        in_specs.append(pl.BlockSpec((1, d), lambda b, i: (0, 0)))
        args.append(g_final.reshape(1, d))
    return pl.pallas_call(
        functools.partial(_ffn_kernel, k0=k0, final=final),
        grid=(bx, t // tm),
        in_specs=in_specs,
        out_specs=pl.BlockSpec((1, tm, d), lambda b, i: (b, i, 0)),
        out_shape=jax.ShapeDtypeStruct(x.shape, f32),
        compiler_params=_cparams(("parallel", "parallel")),
        name="ffn_final" if final else "ffn",
    )(*args)


def _rope(t, cos, sin_signed, first_half):
    w = t.shape[-1]
    partner = jnp.where(first_half, pltpu.roll(t, w - 16, axis=1), pltpu.roll(t, 16, axis=1))
    return t * cos + partner * sin_signed


def _inproj_kernel(x_ref, mod_ref, g_ref, w_ref, *rest, latent):
    if latent:
        cos_ref, sin_ref, qt_ref, k_ref, vt_ref, u_ref = rest
    else:
        k_ref, vt_ref = rest
    x = x_ref[0]
    shift = mod_ref[0, 3:4, :]
    scale = mod_ref[0, 4:5, :]
    h = _norm_mod(x, g_ref[...], shift, scale).astype(bf16)
    hx = jnp.dot(h, w_ref[...], preferred_element_type=f32)
    tm = x.shape[0]
    lane = lax.broadcasted_iota(jnp.int32, (tm, HEAD_DIM), 1)
    map1 = lane < QK_DIM
    if latent:
        cos = jnp.concatenate([cos_ref[...]] * ATT_HEADS, axis=1)
        sin_signed = jnp.concatenate([sin_ref[...]] * ATT_HEADS, axis=1)
        lane_w = lax.broadcasted_iota(jnp.int32, (tm, ATT_WIDTH), 1)
        first_half = (lane_w % 32) < 16
        q = _rope(hx[:, :ATT_WIDTH], cos, sin_signed, first_half) * Q_SCALE
        k = _rope(hx[:, ATT_WIDTH:2 * ATT_WIDTH], cos, sin_signed, first_half)
        v = hx[:, 2 * ATT_WIDTH:3 * ATT_WIDTH]
        u_ref[0] = hx[:, 3 * ATT_WIDTH:]
    else:
        k = hx[:, :ATT_WIDTH]
        v = hx[:, ATT_WIDTH:]
    pad_row = lax.broadcasted_iota(jnp.int32, (VT_ROWS - HEAD_DIM, tm), 0)
    ones_pad = jnp.where(pad_row == 0, 1.0, 0.0).astype(bf16)
    for hh in range(ATT_HEADS):
        hs = slice(hh * HEAD_DIM, (hh + 1) * HEAD_DIM)
        kh = k[:, hs]
        k_ref[0, hh, 0] = jnp.where(map1, kh, 0.0).astype(bf16)
        k_ref[0, hh, 1] = jnp.where(map1, 0.0, kh).astype(bf16)
        vt_ref[0, hh, 0:HEAD_DIM, :] = v[:, hs].T.astype(bf16)
        vt_ref[0, hh, HEAD_DIM:, :] = ones_pad
        if latent:
            qt_ref[0, hh] = q[:, hs].T.astype(bf16)


def _inproj_call(x, mod, row_of_batch, g, w, rope_tabs=None):
    bx, t, d = x.shape
    tm = min(TM_PROJ, t)
    latent = rope_tabs is not None
    in_specs = [
        pl.BlockSpec((1, tm, d), lambda b, i: (b, i, 0)),
        pl.BlockSpec((1, N_MOD, d), lambda b, i: (row_of_batch(b), 0, 0)),
        pl.BlockSpec((1, d), lambda b, i: (0, 0)),
        pl.BlockSpec(w.shape, lambda b, i: (0, 0), pipeline_mode=pl.Buffered(1)),
    ]
    args = [x, mod, g.reshape(1, d), w]
    k_spec = pl.BlockSpec((1, ATT_HEADS, 2, tm, HEAD_DIM), lambda b, i: (b, 0, 0, i, 0))
    k_shape = jax.ShapeDtypeStruct((bx, ATT_HEADS, 2, t, HEAD_DIM), bf16)
    vt_spec = pl.BlockSpec((1, ATT_HEADS, VT_ROWS, tm), lambda b, i: (b, 0, 0, i))
    vt_shape = jax.ShapeDtypeStruct((bx, ATT_HEADS, VT_ROWS, t), bf16)
    if latent:
        in_specs += [pl.BlockSpec((tm, HEAD_DIM), lambda b, i: (i, 0))] * 2
        args += list(rope_tabs)
        out_specs = [pl.BlockSpec((1, ATT_HEADS, HEAD_DIM, tm), lambda b, i: (b, 0, 0, i)),
                     k_spec, vt_spec,
                     pl.BlockSpec((1, tm, POOL_WIDTH), lambda b, i: (b, i, 0))]
        out_shape = [jax.ShapeDtypeStruct((bx, ATT_HEADS, HEAD_DIM, t), bf16), k_shape, vt_shape,
                     jax.ShapeDtypeStruct((bx, t, POOL_WIDTH), f32)]
    else:
        out_specs = [k_spec, vt_spec]
        out_shape = [k_shape, vt_shape]
    return pl.pallas_call(
        functools.partial(_inproj_kernel, latent=latent),
        grid=(bx, t // tm),
        in_specs=in_specs,
        out_specs=out_specs,
        out_shape=out_shape,
        compiler_params=_cparams(("parallel", "parallel")),
        name="inproj_lat" if latent else "inproj_ctx",
    )(*args)


def _attn_kernel(lam_ref, gsub_ref, qt_ref, kl_ref, kc_ref, vtl_ref, vtc_ref, o_ref,
                 s_scr, cm_scr, p_scr, m_scr, acc_scr):
    qt = qt_ref[0, 0]
    tq = qt.shape[1]
    n_lat = kl_ref.shape[3]
    m_scr[...] = jnp.full(m_scr.shape, NEG_BIG, f32)
    acc_scr[...] = jnp.zeros(acc_scr.shape, f32)

    n_col = tq // MXU_COLS
    n_chunks = n_lat // TK_ATT

    def scores(slot, keys, mi, t):
        tk = keys.shape[0]
        ts = slice(t * MXU_COLS, (t + 1) * MXU_COLS)
        st = jnp.dot(keys, qt[:, ts], preferred_element_type=f32)
        s_scr[slot, mi, 0:tk, ts] = st
        cm_scr[slot, mi, :, ts] = jnp.max(st.reshape(tk // 8, 8, MXU_COLS), axis=0)

    def softmax_pv(slot, vt, mi, t):
        tk = vt.shape[1]
        ts = slice(t * MXU_COLS, (t + 1) * MXU_COLS)
        m_old = m_scr[mi, :, ts]
        m_new = jnp.maximum(m_old, jnp.max(cm_scr[slot, mi, :, ts], axis=0, keepdims=True))
        alpha = jnp.exp2(m_old - m_new)
        m_scr[mi, :, ts] = m_new
        for c in range(MXU_COLS // HEAD_DIM):
            cs = slice(t * MXU_COLS + c * HEAD_DIM, t * MXU_COLS + (c + 1) * HEAD_DIM)
            mb = jnp.broadcast_to(m_new[:, c * HEAD_DIM:(c + 1) * HEAD_DIM], (RB_ATT, HEAD_DIM))
            for r in range(tk // RB_ATT):
                rs = slice(r * RB_ATT, (r + 1) * RB_ATT)
                p_scr[mi, rs, cs] = jnp.exp2(s_scr[slot, mi, rs, cs] - mb).astype(bf16)
        pv = jnp.dot(vt, p_scr[mi, 0:tk, ts], preferred_element_type=f32)
        acc_scr[mi, :, ts] = alpha * acc_scr[mi, :, ts] + pv

    def keys_of(j, mi):
        if j < n_chunks:
            return kl_ref[0, 0, mi, j * TK_ATT:(j + 1) * TK_ATT, :]
        return kc_ref[0, 0, mi]

    def values_of(j):
        if j < n_chunks:
            return vtl_ref[0, 0, :, j * TK_ATT:(j + 1) * TK_ATT]
        return vtc_ref[0, 0]

    for mi in range(2):
        for t in range(n_col):
            scores(0, keys_of(0, mi), mi, t)
    for j in range(n_chunks + 1):
        for mi in range(2):
            for t in range(n_col):
                if j < n_chunks:
                    scores((j + 1) % 2, keys_of(j + 1, mi), mi, t)
                softmax_pv(j % 2, values_of(j), mi, t)

    lam_v = lam_ref[...]
    lam = (jnp.exp(jnp.sum(lam_v[0:1] * lam_v[1:2], axis=-1, keepdims=True))
           - jnp.exp(jnp.sum(lam_v[2:3] * lam_v[3:4], axis=-1, keepdims=True))
           + LAMBDA_INIT)
    o1 = acc_scr[0, 0:HEAD_DIM, :] / acc_scr[0, HEAD_DIM:HEAD_DIM + 1, :]
    o2 = acc_scr[1, 0:HEAD_DIM, :] / acc_scr[1, HEAD_DIM:HEAD_DIM + 1, :]
    o = o1 - lam * o2
    ms = jnp.mean(o * o, axis=0, keepdims=True)
    o = o * lax.rsqrt(ms + EPS) * (gsub_ref[...] * (1.0 - LAMBDA_INIT))
    o_ref[0] = o.T.astype(bf16)


def _attn_call(lam_rows, g_sub, qt, k_lat, k_ctx, vt_lat, vt_ctx):
    b, _, _, n = qt.shape
    n_ctx = k_ctx.shape[3]
    tq = TQ_ATT
    return pl.pallas_call(
        _attn_kernel,
        grid=(b, ATT_HEADS, n // tq),
        in_specs=[
            pl.BlockSpec((4, QK_DIM), lambda bb, h, i: (0, 0)),
            pl.BlockSpec((HEAD_DIM, 1), lambda bb, h, i: (0, 0)),
            pl.BlockSpec((1, 1, HEAD_DIM, tq), lambda bb, h, i: (bb, h, 0, i)),
            pl.BlockSpec((1, 1, 2, n, HEAD_DIM), lambda bb, h, i: (bb, h, 0, 0, 0)),
            pl.BlockSpec((1, 1, 2, n_ctx, HEAD_DIM), lambda bb, h, i: (bb, h, 0, 0, 0)),
            pl.BlockSpec((1, 1, VT_ROWS, n), lambda bb, h, i: (bb, h, 0, 0)),
            pl.BlockSpec((1, 1, VT_ROWS, n_ctx), lambda bb, h, i: (bb, h, 0, 0)),
        ],
        out_specs=pl.BlockSpec((1, tq, HEAD_DIM), lambda bb, h, i: (bb, i, h)),
        out_shape=jax.ShapeDtypeStruct((b, n, ATT_WIDTH), bf16),
        scratch_shapes=[
            pltpu.VMEM((2, 2, TK_ATT, tq), f32),
            pltpu.VMEM((2, 2, 8, tq), f32),
            pltpu.VMEM((2, TK_ATT, tq), bf16),
            pltpu.VMEM((2, 1, tq), f32),
            pltpu.VMEM((2, VT_ROWS, tq), f32),
        ],
        compiler_params=_cparams(("parallel", "parallel", "arbitrary")),
        name="attn",
    )(lam_rows, g_sub.reshape(HEAD_DIM, 1), qt, k_lat, k_ctx, vt_lat, vt_ctx)


def _mixout_kernel(x_ref, mod_ref, att_ref, u_ref, up_ref, un_ref, wp_ref, ps_ref, wo_ref,
                   o_ref, ubuf, *, n_tok):
    i = pl.program_id(1)
    nt = pl.num_programs(1)
    tm = x_ref.shape[1]
    hl = POOL_HALO
    ubuf[0:hl, :] = jnp.where(i > 0, up_ref[0], 0.0)
    ubuf[hl:hl + tm, :] = u_ref[0]
    ubuf[hl + tm:, :] = jnp.where(i < nt - 1, un_ref[0], 0.0)

    t = i * tm + lax.broadcasted_iota(jnp.int32, (tm, 1), 0)
    pooled = []
    for gi, w in enumerate(POOL_WINDOWS):
        lo = w // 2
        hi = w - lo - 1
        sl = slice(gi * POOL_GROUP_DIM, (gi + 1) * POOL_GROUP_DIM)
        total = ubuf[hl - lo:hl - lo + tm, sl]
        for d in range(-lo + 1, hi + 1):
            total = total + ubuf[hl + d:hl + d + tm, sl]
        cnt = (jnp.minimum(t + hi, n_tok - 1) - jnp.maximum(t - lo, 0) + 1).astype(f32)
        diff = (total / cnt - ubuf[hl:hl + tm, sl]).astype(bf16)
        pooled.append(jnp.dot(diff, wp_ref[gi], preferred_element_type=f32))
    pool = (jnp.concatenate(pooled, axis=-1) * ps_ref[...]).astype(bf16)
    y = jnp.dot(att_ref[0], wo_ref[0:ATT_WIDTH, :], preferred_element_type=f32)
    y = y + jnp.dot(pool, wo_ref[ATT_WIDTH:, :], preferred_element_type=f32)
    o_ref[0] = x_ref[0] + mod_ref[0, 5:6, :] * y


def _mixout_call(x, mod, att, u, w_pool, pool_scale, w_out):
    b, n, d = x.shape
    tm = TM_MIX
    hb = tm // POOL_HALO
    last_halo_block = n // POOL_HALO - 1
    return pl.pallas_call(
        functools.partial(_mixout_kernel, n_tok=n),
        grid=(b, n // tm),
        in_specs=[
            pl.BlockSpec((1, tm, d), lambda bb, i: (bb, i, 0)),
            pl.BlockSpec((1, N_MOD, d), lambda bb, i: (bb, 0, 0)),
            pl.BlockSpec((1, tm, ATT_WIDTH), lambda bb, i: (bb, i, 0)),
            pl.BlockSpec((1, tm, POOL_WIDTH), lambda bb, i: (bb, i, 0)),
            pl.BlockSpec((1, POOL_HALO, POOL_WIDTH),
                         lambda bb, i: (bb, jnp.maximum(i * hb - 1, 0), 0)),
            pl.BlockSpec((1, POOL_HALO, POOL_WIDTH),
                         lambda bb, i: (bb, jnp.minimum((i + 1) * hb, last_halo_block), 0)),
            pl.BlockSpec(w_pool.shape, lambda bb, i: (0, 0, 0)),
            pl.BlockSpec((1, POOL_WIDTH), lambda bb, i: (0, 0)),
            pl.BlockSpec(w_out.shape, lambda bb, i: (0, 0)),
        ],
        out_specs=pl.BlockSpec((1, tm, d), lambda bb, i: (bb, i, 0)),
        out_shape=jax.ShapeDtypeStruct(x.shape, f32),
        scratch_shapes=[pltpu.VMEM((tm + 2 * POOL_HALO, POOL_WIDTH), f32)],
        compiler_params=_cparams(("parallel", "parallel")),
        name="mixout",
    )(x, mod, att, u, u, u, w_pool, pool_scale.reshape(1, POOL_WIDTH), w_out)


def _rope_tables(n):
    rows = n // GRID_W
    row = jnp.repeat(jnp.arange(rows, dtype=f32), GRID_W)
    col = jnp.tile(jnp.arange(GRID_W, dtype=f32), rows)
    nf = QK_DIM // 4
    freqs = ROPE_BASE ** (-jnp.arange(nf, dtype=f32) / nf)
    ar = row[:, None] * freqs
    ac = col[:, None] * freqs
    ang = jnp.concatenate([ar, ar, ac, ac], axis=-1)
    sign = jnp.tile(jnp.concatenate([-jnp.ones((nf,), f32), jnp.ones((nf,), f32)]), 2)
    cos = jnp.tile(jnp.cos(ang), (1, 2))
    sin_signed = jnp.tile(jnp.sin(ang) * sign, (1, 2))
    return cos, sin_signed


def kernel(x, c, ctx, c_ctx, w_mod, b_mod, g_ffn1, ffn1_w_in, ffn1_w_out, g_mix, w_in,
           lambda_q1, lambda_k1, lambda_q2, lambda_k2, g_sub, w_pool, pool_scale, w_out,
           g_ffn2, ffn2_w_in, ffn2_w_out, g_final):
    b, n, d = x.shape
    l = 0
    cond = jnp.zeros((MOD_ROWS, d), f32).at[:b].set(c).at[b].set(c_ctx)
    mod = _mod_call(cond, w_mod[l], b_mod[l]).reshape(MOD_ROWS, N_MOD, d)
    lat_row = lambda bb: bb
    ctx_row = lambda bb: b

    w1_in, w1_out = ffn1_w_in[l].astype(bf16), ffn1_w_out[l].astype(bf16)
    w2_in, w2_out = ffn2_w_in[l].astype(bf16), ffn2_w_out[l].astype(bf16)
    w_in_b = w_in[l].astype(bf16)
    w_o_b = w_out[l].astype(bf16)
    w_pool_b = w_pool[l].astype(bf16)

    x1 = _ffn_call(x, mod, lat_row, 0, g_ffn1[l], w1_in, w1_out)
    cx1 = _ffn_call(ctx, mod, ctx_row, 0, g_ffn1[l], w1_in, w1_out)

    qt, k_lat, vt_lat, u = _inproj_call(x1, mod, lat_row, g_mix[l], w_in_b, _rope_tables(n))
    k_ctx, vt_ctx = _inproj_call(cx1, mod, ctx_row, g_mix[l],
                                 w_in_b[:, ATT_WIDTH:3 * ATT_WIDTH])

    lam_rows = jnp.stack([lambda_q1[l], lambda_k1[l], lambda_q2[l], lambda_k2[l]]).astype(f32)
    att = _attn_call(lam_rows, g_sub[l], qt, k_lat, k_ctx, vt_lat, vt_ctx)

    x2 = _mixout_call(x1, mod, att, u, w_pool_b, pool_scale[l], w_o_b)
    return _ffn_call(x2, mod, lat_row, 6, g_ffn2[l], w2_in, w2_out, g_final)
```

```python
import functools
import math

import jax
import jax.numpy as jnp
from jax import lax
from jax.experimental import pallas as pl
from jax.experimental.pallas import tpu as pltpu

D_MODEL = 1024
GRID_W = 64
ATT_WIDTH = 512
POOL_WIDTH = 512
ATT_HEADS = 4
HEAD_DIM = 128
QK_DIM = 64
POOL_WINDOWS = (2, 4, 8, 16)
POOL_GROUP_DIM = 128
D_FF = 2816
N_MOD = 9
ROPE_BASE = 10000.0
EPS = 1e-6
LAMBDA_INIT = 0.8 - 0.6 * math.exp(-0.3 * 0)

MXU_COLS = 256
SLABS_PER_TILE = MXU_COLS // HEAD_DIM
MOD_ROWS = 16
POOL_HALO = 8
NEG_BIG = -1e30
Q_SCALE = (QK_DIM ** -0.5) * math.log2(math.e)

VMEM_LIMIT = 56 * 1024 * 1024

TM_FFN = 512
TM_PROJ = 512
TM_MIX = 512
TQ_ATT = 512
TK_ATT = 1024
RB_ATT = 64
VT_ROWS = HEAD_DIM + 16
FF_CHUNKS = (512, 512, 512, 512, 512, 256)

bf16 = jnp.bfloat16
f32 = jnp.float32


def _cparams(sem):
    return pltpu.CompilerParams(dimension_semantics=sem, vmem_limit_bytes=VMEM_LIMIT)


def _norm_mod(x, g, shift, scale):
    ms = jnp.mean(x * x, axis=-1, keepdims=True)
    y = x * lax.rsqrt(ms + EPS) * g
    return y * (1.0 + scale) + shift


def _mod_kernel(cond_ref, w_ref, b_ref, o_ref):
    cnd = cond_ref[...]
    a = (cnd * jax.nn.sigmoid(cnd)).astype(bf16)
    o_ref[...] = jnp.dot(a, w_ref[...].astype(bf16), preferred_element_type=f32) + b_ref[...]


def _mod_call(cond, w_mod, b_mod):
    n_out = w_mod.shape[1]
    tn = D_MODEL
    return pl.pallas_call(
        _mod_kernel,
        grid=(n_out // tn,),
        in_specs=[
            pl.BlockSpec((MOD_ROWS, D_MODEL), lambda j: (0, 0)),
            pl.BlockSpec((D_MODEL, tn), lambda j: (0, j)),
            pl.BlockSpec((1, tn), lambda j: (0, j)),
        ],
        out_specs=pl.BlockSpec((MOD_ROWS, tn), lambda j: (0, j)),
        out_shape=jax.ShapeDtypeStruct((MOD_ROWS, n_out), f32),
        compiler_params=_cparams(("arbitrary",)),
        name="mod",
    )(cond, w_mod, b_mod.reshape(1, n_out))


def _ffn_kernel(x_ref, mod_ref, g_ref, win_ref, wout_ref, *rest, k0, final):
    if final:
        gf_ref, o_ref = rest
    else:
        (o_ref,) = rest
    x = x_ref[0]
    shift = mod_ref[0, k0:k0 + 1, :]
    scale = mod_ref[0, k0 + 1:k0 + 2, :]
    gate = mod_ref[0, k0 + 2:k0 + 3, :]
    h = _norm_mod(x, g_ref[...], shift, scale).astype(bf16)
    acc = jnp.zeros(x.shape, f32)
    off = 0
    for tf in FF_CHUNKS:
        gt = jnp.dot(h, win_ref[:, off:off + tf], preferred_element_type=f32)
        up = jnp.dot(h, win_ref[:, D_FF + off:D_FF + off + tf], preferred_element_type=f32)
        a = (gt * jax.nn.sigmoid(gt) * up).astype(bf16)
        acc = acc + jnp.dot(a, wout_ref[off:off + tf, :], preferred_element_type=f32)
        off += tf
    y = x + 0.5 * gate * acc
    if final:
        ms = jnp.mean(y * y, axis=-1, keepdims=True)
        y = y * lax.rsqrt(ms + EPS) * gf_ref[...]
    o_ref[0] = y


def _ffn_call(x, mod, row_of_batch, k0, g, w_in, w_out, g_final=None):
    bx, t, d = x.shape
    tm = min(TM_FFN, t)
    final = g_final is not None
    in_specs = [
        pl.BlockSpec((1, tm, d), lambda b, i: (b, i, 0)),
        pl.BlockSpec((1, N_MOD, d), lambda b, i: (row_of_batch(b), 0, 0)),
        pl.BlockSpec((1, d), lambda b, i: (0, 0)),
        pl.BlockSpec(w_in.shape, lambda b, i: (0, 0), pipeline_mode=pl.Buffered(1)),
        pl.BlockSpec(w_out.shape, lambda b, i: (0, 0), pipeline_mode=pl.Buffered(1)),
    ]
    args = [x, mod, g.reshape(1, d), w_in, w_out]
    if final:
        in_specs.append(pl.BlockSpec((1, d), lambda b, i: (0, 0)))
        args.append(g_final.reshape(1, d))
    return pl.pallas_call(
        functools.partial(_ffn_kernel, k0=k0, final=final),
        grid=(bx, t // tm),
        in_specs=in_specs,
        out_specs=pl.BlockSpec((1, tm, d), lambda b, i: (b, i, 0)),
        out_shape=jax.ShapeDtypeStruct(x.shape, f32),
        compiler_params=_cparams(("parallel", "parallel")),
        name="ffn_final" if final else "ffn",
    )(*args)


def _rope(t, cos, sin_signed, first_half):
    w = t.shape[-1]
    partner = jnp.where(first_half, pltpu.roll(t, w - 16, axis=1), pltpu.roll(t, 16, axis=1))
    return t * cos + partner * sin_signed


def _inproj_kernel(x_ref, mod_ref, g_ref, w_ref, *rest, latent):
    if latent:
        cos_ref, sin_ref, qt_ref, k_ref, vt_ref, u_ref = rest
    else:
        k_ref, vt_ref = rest
    x = x_ref[0]
    shift = mod_ref[0, 3:4, :]
    scale = mod_ref[0, 4:5, :]
    h = _norm_mod(x, g_ref[...], shift, scale).astype(bf16)
    hx = jnp.dot(h, w_ref[...], preferred_element_type=f32)
    tm = x.shape[0]
    lane = lax.broadcasted_iota(jnp.int32, (tm, HEAD_DIM), 1)
    map1 = lane < QK_DIM
    if latent:
        cos = jnp.concatenate([cos_ref[...]] * ATT_HEADS, axis=1)
        sin_signed = jnp.concatenate([sin_ref[...]] * ATT_HEADS, axis=1)
        lane_w = lax.broadcasted_iota(jnp.int32, (tm, ATT_WIDTH), 1)
        first_half = (lane_w % 32) < 16
        q = _rope(hx[:, :ATT_WIDTH], cos, sin_signed, first_half) * Q_SCALE
        k = _rope(hx[:, ATT_WIDTH:2 * ATT_WIDTH], cos, sin_signed, first_half)
        v = hx[:, 2 * ATT_WIDTH:3 * ATT_WIDTH]
        u_ref[0] = hx[:, 3 * ATT_WIDTH:]
    else:
        k = hx[:, :ATT_WIDTH]
        v = hx[:, ATT_WIDTH:]
    pad_row = lax.broadcasted_iota(jnp.int32, (VT_ROWS - HEAD_DIM, tm), 0)
    ones_pad = jnp.where(pad_row == 0, 1.0, 0.0).astype(bf16)
    for hh in range(ATT_HEADS):
        hs = slice(hh * HEAD_DIM, (hh + 1) * HEAD_DIM)
        kh = k[:, hs]
        k_ref[0, hh, 0] = jnp.where(map1, kh, 0.0).astype(bf16)
        k_ref[0, hh, 1] = jnp.where(map1, 0.0, kh).astype(bf16)
        vt_ref[0, hh, 0:HEAD_DIM, :] = v[:, hs].T.astype(bf16)
        vt_ref[0, hh, HEAD_DIM:, :] = ones_pad
        if latent:
            qt_ref[0, hh] = q[:, hs].T.astype(bf16)


def _inproj_call(x, mod, row_of_batch, g, w, rope_tabs=None):
    bx, t, d = x.shape
    tm = min(TM_PROJ, t)
    latent = rope_tabs is not None
    in_specs = [
        pl.BlockSpec((1, tm, d), lambda b, i: (b, i, 0)),
        pl.BlockSpec((1, N_MOD, d), lambda b, i: (row_of_batch(b), 0, 0)),
        pl.BlockSpec((1, d), lambda b, i: (0, 0)),
        pl.BlockSpec(w.shape, lambda b, i: (0, 0), pipeline_mode=pl.Buffered(1)),
    ]
    args = [x, mod, g.reshape(1, d), w]
    k_spec = pl.BlockSpec((1, ATT_HEADS, 2, tm, HEAD_DIM), lambda b, i: (b, 0, 0, i, 0))
    k_shape = jax.ShapeDtypeStruct((bx, ATT_HEADS, 2, t, HEAD_DIM), bf16)
    vt_spec = pl.BlockSpec((1, ATT_HEADS, VT_ROWS, tm), lambda b, i: (b, 0, 0, i))
    vt_shape = jax.ShapeDtypeStruct((bx, ATT_HEADS, VT_ROWS, t), bf16)
    if latent:
        in_specs += [pl.BlockSpec((tm, HEAD_DIM), lambda b, i: (i, 0))] * 2
        args += list(rope_tabs)
        out_specs = [pl.BlockSpec((1, ATT_HEADS, HEAD_DIM, tm), lambda b, i: (b, 0, 0, i)),
                     k_spec, vt_spec,
                     pl.BlockSpec((1, tm, POOL_WIDTH), lambda b, i: (b, i, 0))]
        out_shape = [jax.ShapeDtypeStruct((bx, ATT_HEADS, HEAD_DIM, t), bf16), k_shape, vt_shape,
                     jax.ShapeDtypeStruct((bx, t, POOL_WIDTH), f32)]
    else:
        out_specs = [k_spec, vt_spec]
        out_shape = [k_shape, vt_shape]
    return pl.pallas_call(
        functools.partial(_inproj_kernel, latent=latent),
        grid=(bx, t // tm),
        in_specs=in_specs,
        out_specs=out_specs,
        out_shape=out_shape,
        compiler_params=_cparams(("parallel", "parallel")),
        name="inproj_lat" if latent else "inproj_ctx",
    )(*args)


def _attn_kernel(lam_ref, gsub_ref, qt_ref, kl_ref, kc_ref, vtl_ref, vtc_ref, o_ref,
                 s_scr, cm_scr, p_scr, m_scr, acc_scr):
    qt = qt_ref[0, 0]
    tq = qt.shape[1]
    n_lat = kl_ref.shape[3]
    m_scr[...] = jnp.full(m_scr.shape, NEG_BIG, f32)
    acc_scr[...] = jnp.zeros(acc_scr.shape, f32)

    n_col = tq // MXU_COLS
    n_chunks = n_lat // TK_ATT

    def scores(slot, keys, mi, t):
        tk = keys.shape[0]
        ts = slice(t * MXU_COLS, (t + 1) * MXU_COLS)
        st = jnp.dot(keys, qt[:, ts], preferred_element_type=f32)
        for c in range(SLABS_PER_TILE):
            s_scr[slot, mi, t * SLABS_PER_TILE + c, 0:tk, :] = (
                st[:, c * HEAD_DIM:(c + 1) * HEAD_DIM])
        cm_scr[slot, mi, :, ts] = jnp.max(st.reshape(tk // 8, 8, MXU_COLS), axis=0)

    def softmax_pv(slot, vt, mi, t):
        tk = vt.shape[1]
        ts = slice(t * MXU_COLS, (t + 1) * MXU_COLS)
        m_old = m_scr[mi, :, ts]
        m_new = jnp.maximum(m_old, jnp.max(cm_scr[slot, mi, :, ts], axis=0, keepdims=True))
        alpha = jnp.exp2(m_old - m_new)
        m_scr[mi, :, ts] = m_new
        for c in range(SLABS_PER_TILE):
            sb = t * SLABS_PER_TILE + c
            mb = jnp.broadcast_to(m_new[:, c * HEAD_DIM:(c + 1) * HEAD_DIM], (RB_ATT, HEAD_DIM))
            for r in range(tk // RB_ATT):
                rs = slice(r * RB_ATT, (r + 1) * RB_ATT)
                p_scr[mi, sb, rs, :] = jnp.exp2(s_scr[slot, mi, sb, rs, :] - mb).astype(bf16)
        p = jnp.concatenate([p_scr[mi, t * SLABS_PER_TILE + c, 0:tk, :]
                             for c in range(SLABS_PER_TILE)], axis=1)
        pv = jnp.dot(vt, p, preferred_element_type=f32)
        acc_scr[mi, :, ts] = alpha * acc_scr[mi, :, ts] + pv

    def keys_of(j, mi):
        if j < n_chunks:
            return kl_ref[0, 0, mi, j * TK_ATT:(j + 1) * TK_ATT, :]
        return kc_ref[0, 0, mi]

    def values_of(j):
        if j < n_chunks:
            return vtl_ref[0, 0, :, j * TK_ATT:(j + 1) * TK_ATT]
        return vtc_ref[0, 0]

    for mi in range(2):
        for t in range(n_col):
            scores(0, keys_of(0, mi), mi, t)
    for j in range(n_chunks + 1):
        for mi in range(2):
            for t in range(n_col):
                if j < n_chunks:
                    scores((j + 1) % 2, keys_of(j + 1, mi), mi, t)
                softmax_pv(j % 2, values_of(j), mi, t)

    lam_v = lam_ref[...]
    lam = (jnp.exp(jnp.sum(lam_v[0:1] * lam_v[1:2], axis=-1, keepdims=True))
           - jnp.exp(jnp.sum(lam_v[2:3] * lam_v[3:4], axis=-1, keepdims=True))
           + LAMBDA_INIT)
    o1 = acc_scr[0, 0:HEAD_DIM, :] / acc_scr[0, HEAD_DIM:HEAD_DIM + 1, :]
    o2 = acc_scr[1, 0:HEAD_DIM, :] / acc_scr[1, HEAD_DIM:HEAD_DIM + 1, :]
    o = o1 - lam * o2
    ms = jnp.mean(o * o, axis=0, keepdims=True)
    o = o * lax.rsqrt(ms + EPS) * (gsub_ref[...] * (1.0 - LAMBDA_INIT))
    o_ref[0] = o.T.astype(bf16)


def _attn_call(lam_rows, g_sub, qt, k_lat, k_ctx, vt_lat, vt_ctx):
    b, _, _, n = qt.shape
    n_ctx = k_ctx.shape[3]
    tq = TQ_ATT
    return pl.pallas_call(
        _attn_kernel,
        grid=(b, ATT_HEADS, n // tq),
        in_specs=[
            pl.BlockSpec((4, QK_DIM), lambda bb, h, i: (0, 0)),
            pl.BlockSpec((HEAD_DIM, 1), lambda bb, h, i: (0, 0)),
            pl.BlockSpec((1, 1, HEAD_DIM, tq), lambda bb, h, i: (bb, h, 0, i)),
            pl.BlockSpec((1, 1, 2, n, HEAD_DIM), lambda bb, h, i: (bb, h, 0, 0, 0)),
            pl.BlockSpec((1, 1, 2, n_ctx, HEAD_DIM), lambda bb, h, i: (bb, h, 0, 0, 0)),
            pl.BlockSpec((1, 1, VT_ROWS, n), lambda bb, h, i: (bb, h, 0, 0)),
            pl.BlockSpec((1, 1, VT_ROWS, n_ctx), lambda bb, h, i: (bb, h, 0, 0)),
        ],
        out_specs=pl.BlockSpec((1, tq, HEAD_DIM), lambda bb, h, i: (bb, i, h)),
        out_shape=jax.ShapeDtypeStruct((b, n, ATT_WIDTH), bf16),
        scratch_shapes=[
            pltpu.VMEM((2, 2, tq // HEAD_DIM, TK_ATT, HEAD_DIM), f32),
            pltpu.VMEM((2, 2, 8, tq), f32),
            pltpu.VMEM((2, tq // HEAD_DIM, TK_ATT, HEAD_DIM), bf16),
            pltpu.VMEM((2, 1, tq), f32),
            pltpu.VMEM((2, VT_ROWS, tq), f32),
        ],
        compiler_params=_cparams(("parallel", "parallel", "arbitrary")),
        name="attn",
    )(lam_rows, g_sub.reshape(HEAD_DIM, 1), qt, k_lat, k_ctx, vt_lat, vt_ctx)


def _mixout_kernel(x_ref, mod_ref, att_ref, u_ref, up_ref, un_ref, wp_ref, ps_ref, wo_ref,
                   o_ref, ubuf, *, n_tok):
    i = pl.program_id(1)
    nt = pl.num_programs(1)
    tm = x_ref.shape[1]
    hl = POOL_HALO
    ubuf[0:hl, :] = jnp.where(i > 0, up_ref[0], 0.0)
    ubuf[hl:hl + tm, :] = u_ref[0]
    ubuf[hl + tm:, :] = jnp.where(i < nt - 1, un_ref[0], 0.0)

    t = i * tm + lax.broadcasted_iota(jnp.int32, (tm, 1), 0)
    pooled = []
    for gi, w in enumerate(POOL_WINDOWS):
        lo = w // 2
        hi = w - lo - 1
        sl = slice(gi * POOL_GROUP_DIM, (gi + 1) * POOL_GROUP_DIM)
        total = ubuf[hl - lo:hl - lo + tm, sl]
        for d in range(-lo + 1, hi + 1):
            total = total + ubuf[hl + d:hl + d + tm, sl]
        cnt = (jnp.minimum(t + hi, n_tok - 1) - jnp.maximum(t - lo, 0) + 1).astype(f32)
        diff = (total / cnt - ubuf[hl:hl + tm, sl]).astype(bf16)
        pooled.append(jnp.dot(diff, wp_ref[gi], preferred_element_type=f32))
    pool = (jnp.concatenate(pooled, axis=-1) * ps_ref[...]).astype(bf16)
    y = jnp.dot(att_ref[0], wo_ref[0:ATT_WIDTH, :], preferred_element_type=f32)
    y = y + jnp.dot(pool, wo_ref[ATT_WIDTH:, :], preferred_element_type=f32)
    o_ref[0] = x_ref[0] + mod_ref[0, 5:6, :] * y


def _mixout_call(x, mod, att, u, w_pool, pool_scale, w_out):
    b, n, d = x.shape
    tm = TM_MIX
    hb = tm // POOL_HALO
    last_halo_block = n // POOL_HALO - 1
    return pl.pallas_call(
        functools.partial(_mixout_kernel, n_tok=n),
        grid=(b, n // tm),
        in_specs=[
            pl.BlockSpec((1, tm, d), lambda bb, i: (bb, i, 0)),
            pl.BlockSpec((1, N_MOD, d), lambda bb, i: (bb, 0, 0)),
            pl.BlockSpec((1, tm, ATT_WIDTH), lambda bb, i: (bb, i, 0)),
            pl.BlockSpec((1, tm, POOL_WIDTH), lambda bb, i: (bb, i, 0)),
            pl.BlockSpec((1, POOL_HALO, POOL_WIDTH),
                         lambda bb, i: (bb, jnp.maximum(i * hb - 1, 0), 0)),
            pl.BlockSpec((1, POOL_HALO, POOL_WIDTH),
                         lambda bb, i: (bb, jnp.minimum((i + 1) * hb, last_halo_block), 0)),
            pl.BlockSpec(w_pool.shape, lambda bb, i: (0, 0, 0)),
            pl.BlockSpec((1, POOL_WIDTH), lambda bb, i: (0, 0)),
            pl.BlockSpec(w_out.shape, lambda bb, i: (0, 0)),
        ],
        out_specs=pl.BlockSpec((1, tm, d), lambda bb, i: (bb, i, 0)),
        out_shape=jax.ShapeDtypeStruct(x.shape, f32),
        scratch_shapes=[pltpu.VMEM((tm + 2 * POOL_HALO, POOL_WIDTH), f32)],
        compiler_params=_cparams(("parallel", "parallel")),
        name="mixout",
    )(x, mod, att, u, u, u, w_pool, pool_scale.reshape(1, POOL_WIDTH), w_out)


def _rope_tables(n):
    rows = n // GRID_W
    row = jnp.repeat(jnp.arange(rows, dtype=f32), GRID_W)
    col = jnp.tile(jnp.arange(GRID_W, dtype=f32), rows)
    nf = QK_DIM // 4
    freqs = ROPE_BASE ** (-jnp.arange(nf, dtype=f32) / nf)
    ar = row[:, None] * freqs
    ac = col[:, None] * freqs
    ang = jnp.concatenate([ar, ar, ac, ac], axis=-1)
    sign = jnp.tile(jnp.concatenate([-jnp.ones((nf,), f32), jnp.ones((nf,), f32)]), 2)
    cos = jnp.tile(jnp.cos(ang), (1, 2))
    sin_signed = jnp.tile(jnp.sin(ang) * sign, (1, 2))
    return cos, sin_signed


def kernel(x, c, ctx, c_ctx, w_mod, b_mod, g_ffn1, ffn1_w_in, ffn1_w_out, g_mix, w_in,
           lambda_q1, lambda_k1, lambda_q2, lambda_k2, g_sub, w_pool, pool_scale, w_out,
           g_ffn2, ffn2_w_in, ffn2_w_out, g_final):
    b, n, d = x.shape
    l = 0
    cond = jnp.zeros((MOD_ROWS, d), f32).at[:b].set(c).at[b].set(c_ctx)
    mod = _mod_call(cond, w_mod[l], b_mod[l]).reshape(MOD_ROWS, N_MOD, d)
    lat_row = lambda bb: bb
    ctx_row = lambda bb: b

    w1_in, w1_out = ffn1_w_in[l].astype(bf16), ffn1_w_out[l].astype(bf16)
    w2_in, w2_out = ffn2_w_in[l].astype(bf16), ffn2_w_out[l].astype(bf16)
    w_in_b = w_in[l].astype(bf16)
    w_o_b = w_out[l].astype(bf16)
    w_pool_b = w_pool[l].astype(bf16)

    x1 = _ffn_call(x, mod, lat_row, 0, g_ffn1[l], w1_in, w1_out)
    cx1 = _ffn_call(ctx, mod, ctx_row, 0, g_ffn1[l], w1_in, w1_out)

    qt, k_lat, vt_lat, u = _inproj_call(x1, mod, lat_row, g_mix[l], w_in_b, _rope_tables(n))
    k_ctx, vt_ctx = _inproj_call(cx1, mod, ctx_row, g_mix[l],
                                 w_in_b[:, ATT_WIDTH:3 * ATT_WIDTH])

    lam_rows = jnp.stack([lambda_q1[l], lambda_k1[l], lambda_q2[l], lambda_k2[l]]).astype(f32)
    att = _attn_call(lam_rows, g_sub[l], qt, k_lat, k_ctx, vt_lat, vt_ctx)

    x2 = _mixout_call(x1, mod, att, u, w_pool_b, pool_scale[l], w_o_b)
    return _ffn_call(x2, mod, lat_row, 6, g_ffn2[l], w2_in, w2_out, g_final)
```

```python
import functools
import math

import jax
import jax.numpy as jnp
from jax import lax
from jax.experimental import pallas as pl
from jax.experimental.pallas import tpu as pltpu

D_MODEL = 1024
GRID_W = 64
ATT_WIDTH = 512
POOL_WIDTH = 512
ATT_HEADS = 4
HEAD_DIM = 128
QK_DIM = 64
POOL_WINDOWS = (2, 4, 8, 16)
POOL_GROUP_DIM = 128
D_FF = 2816
N_MOD = 9
ROPE_BASE = 10000.0
EPS = 1e-6
LAMBDA_INIT = 0.8 - 0.6 * math.exp(-0.3 * 0)

MXU_COLS = 256
SLABS_PER_TILE = MXU_COLS // HEAD_DIM
MOD_ROWS = 16
POOL_HALO = 8
NEG_BIG = -1e30
Q_SCALE = (QK_DIM ** -0.5) * math.log2(math.e)

VMEM_LIMIT = 56 * 1024 * 1024

TM_FFN = 512
TM_PROJ = 512
TM_MIX = 512
TQ_ATT = 512
TK_ATT = 1024
VT_ROWS = HEAD_DIM + 16
FF_CHUNKS = (512, 512, 512, 512, 512, 256)

bf16 = jnp.bfloat16
f32 = jnp.float32


def _cparams(sem, flags=None):
    return pltpu.CompilerParams(dimension_semantics=sem, vmem_limit_bytes=VMEM_LIMIT,
                                flags=flags)


def _norm_mod(x, g, shift, scale):
    ms = jnp.mean(x * x, axis=-1, keepdims=True)
    y = x * lax.rsqrt(ms + EPS) * g
    return y * (1.0 + scale) + shift


def _mod_kernel(cond_ref, w_ref, b_ref, o_ref):
    cnd = cond_ref[...]
    a = (cnd * jax.nn.sigmoid(cnd)).astype(bf16)
    o_ref[...] = jnp.dot(a, w_ref[...].astype(bf16), preferred_element_type=f32) + b_ref[...]


def _mod_call(cond, w_mod, b_mod):
    n_out = w_mod.shape[1]
    tn = D_MODEL
    return pl.pallas_call(
        _mod_kernel,
        grid=(n_out // tn,),
        in_specs=[
            pl.BlockSpec((MOD_ROWS, D_MODEL), lambda j: (0, 0)),
            pl.BlockSpec((D_MODEL, tn), lambda j: (0, j)),
            pl.BlockSpec((1, tn), lambda j: (0, j)),
        ],
        out_specs=pl.BlockSpec((MOD_ROWS, tn), lambda j: (0, j)),
        out_shape=jax.ShapeDtypeStruct((MOD_ROWS, n_out), f32),
        compiler_params=_cparams(("arbitrary",)),
        name="mod",
    )(cond, w_mod, b_mod.reshape(1, n_out))


def _ffn_kernel(x_ref, mod_ref, g_ref, win_ref, wout_ref, *rest, k0, final):
    if final:
        gf_ref, o_ref = rest
    else:
        (o_ref,) = rest
    x = x_ref[0]
    shift = mod_ref[0, k0:k0 + 1, :]
    scale = mod_ref[0, k0 + 1:k0 + 2, :]
    gate = mod_ref[0, k0 + 2:k0 + 3, :]
    h = _norm_mod(x, g_ref[...], shift, scale).astype(bf16)
    acc = jnp.zeros(x.shape, f32)
    off = 0
    for tf in FF_CHUNKS:
        gt = jnp.dot(h, win_ref[:, off:off + tf], preferred_element_type=f32)
        up = jnp.dot(h, win_ref[:, D_FF + off:D_FF + off + tf], preferred_element_type=f32)
        a = (gt * jax.nn.sigmoid(gt) * up).astype(bf16)
        acc = acc + jnp.dot(a, wout_ref[off:off + tf, :], preferred_element_type=f32)
        off += tf
    y = x + 0.5 * gate * acc
    if final:
        ms = jnp.mean(y * y, axis=-1, keepdims=True)
        y = y * lax.rsqrt(ms + EPS) * gf_ref[...]
    o_ref[0] = y


def _ffn_call(x, mod, row_of_batch, k0, g, w_in, w_out, g_final=None):
    bx, t, d = x.shape
    tm = min(TM_FFN, t)
    final = g_final is not None
    in_specs = [
        pl.BlockSpec((1, tm, d), lambda b, i: (b, i, 0)),
        pl.BlockSpec((1, N_MOD, d), lambda b, i: (row_of_batch(b), 0, 0)),
        pl.BlockSpec((1, d), lambda b, i: (0, 0)),
        pl.BlockSpec(w_in.shape, lambda b, i: (0, 0), pipeline_mode=pl.Buffered(1)),
        pl.BlockSpec(w_out.shape, lambda b, i: (0, 0), pipeline_mode=pl.Buffered(1)),
    ]
    args = [x, mod, g.reshape(1, d), w_in, w_out]
    if final:
        in_specs.append(pl.BlockSpec((1, d), lambda b, i: (0, 0)))
        args.append(g_final.reshape(1, d))
    return pl.pallas_call(
        functools.partial(_ffn_kernel, k0=k0, final=final),
        grid=(bx, t // tm),
        in_specs=in_specs,
        out_specs=pl.BlockSpec((1, tm, d), lambda b, i: (b, i, 0)),
        out_shape=jax.ShapeDtypeStruct(x.shape, f32),
        compiler_params=_cparams(("parallel", "parallel")),
        name="ffn_final" if final else "ffn",
    )(*args)


def _rope(t, cos, sin_signed, first_half):
    w = t.shape[-1]
    partner = jnp.where(first_half, pltpu.roll(t, w - 16, axis=1), pltpu.roll(t, 16, axis=1))
    return t * cos + partner * sin_signed


def _inproj_kernel(x_ref, mod_ref, g_ref, w_ref, *rest, latent):
    if latent:
        cos_ref, sin_ref, qt_ref, k_ref, vt_ref, u_ref = rest
    else:
        k_ref, vt_ref = rest
    x = x_ref[0]
    shift = mod_ref[0, 3:4, :]
    scale = mod_ref[0, 4:5, :]
    h = _norm_mod(x, g_ref[...], shift, scale).astype(bf16)
    hx = jnp.dot(h, w_ref[...], preferred_element_type=f32)
    tm = x.shape[0]
    lane = lax.broadcasted_iota(jnp.int32, (tm, HEAD_DIM), 1)
    map1 = lane < QK_DIM
    if latent:
        cos = jnp.concatenate([cos_ref[...]] * ATT_HEADS, axis=1)
        sin_signed = jnp.concatenate([sin_ref[...]] * ATT_HEADS, axis=1)
        lane_w = lax.broadcasted_iota(jnp.int32, (tm, ATT_WIDTH), 1)
        first_half = (lane_w % 32) < 16
        q = _rope(hx[:, :ATT_WIDTH], cos, sin_signed, first_half) * Q_SCALE
        k = _rope(hx[:, ATT_WIDTH:2 * ATT_WIDTH], cos, sin_signed, first_half)
        v = hx[:, 2 * ATT_WIDTH:3 * ATT_WIDTH]
        u_ref[0] = hx[:, 3 * ATT_WIDTH:]
    else:
        k = hx[:, :ATT_WIDTH]
        v = hx[:, ATT_WIDTH:]
    pad_row = lax.broadcasted_iota(jnp.int32, (VT_ROWS - HEAD_DIM, tm), 0)
    ones_pad = jnp.where(pad_row == 0, 1.0, 0.0).astype(bf16)
    for hh in range(ATT_HEADS):
        hs = slice(hh * HEAD_DIM, (hh + 1) * HEAD_DIM)
        kh = k[:, hs]
        k_ref[0, hh, 0] = jnp.where(map1, kh, 0.0).astype(bf16)
        k_ref[0, hh, 1] = jnp.where(map1, 0.0, kh).astype(bf16)
        vt_ref[0, hh, 0:HEAD_DIM, :] = v[:, hs].T.astype(bf16)
        vt_ref[0, hh, HEAD_DIM:, :] = ones_pad
        if latent:
            qt_ref[0, hh] = q[:, hs].T.astype(bf16)


def _inproj_call(x, mod, row_of_batch, g, w, rope_tabs=None):
    bx, t, d = x.shape
    tm = min(TM_PROJ, t)
    latent = rope_tabs is not None
    in_specs = [
        pl.BlockSpec((1, tm, d), lambda b, i: (b, i, 0)),
        pl.BlockSpec((1, N_MOD, d), lambda b, i: (row_of_batch(b), 0, 0)),
        pl.BlockSpec((1, d), lambda b, i: (0, 0)),
        pl.BlockSpec(w.shape, lambda b, i: (0, 0), pipeline_mode=pl.Buffered(1)),
    ]
    args = [x, mod, g.reshape(1, d), w]
    k_spec = pl.BlockSpec((1, ATT_HEADS, 2, tm, HEAD_DIM), lambda b, i: (b, 0, 0, i, 0))
    k_shape = jax.ShapeDtypeStruct((bx, ATT_HEADS, 2, t, HEAD_DIM), bf16)
    vt_spec = pl.BlockSpec((1, ATT_HEADS, VT_ROWS, tm), lambda b, i: (b, 0, 0, i))
    vt_shape = jax.ShapeDtypeStruct((bx, ATT_HEADS, VT_ROWS, t), bf16)
    if latent:
        in_specs += [pl.BlockSpec((tm, HEAD_DIM), lambda b, i: (i, 0))] * 2
        args += list(rope_tabs)
        out_specs = [pl.BlockSpec((1, ATT_HEADS, HEAD_DIM, tm), lambda b, i: (b, 0, 0, i)),
                     k_spec, vt_spec,
                     pl.BlockSpec((1, tm, POOL_WIDTH), lambda b, i: (b, i, 0))]
        out_shape = [jax.ShapeDtypeStruct((bx, ATT_HEADS, HEAD_DIM, t), bf16), k_shape, vt_shape,
                     jax.ShapeDtypeStruct((bx, t, POOL_WIDTH), f32)]
    else:
        out_specs = [k_spec, vt_spec]
        out_shape = [k_shape, vt_shape]
    return pl.pallas_call(
        functools.partial(_inproj_kernel, latent=latent),
        grid=(bx, t // tm),
        in_specs=in_specs,
        out_specs=out_specs,
        out_shape=out_shape,
        compiler_params=_cparams(("parallel", "parallel")),
        name="inproj_lat" if latent else "inproj_ctx",
    )(*args)


def _attn_kernel(lam_ref, gsub_ref, qt_ref, kl_ref, kc_ref, vtl_ref, vtc_ref, o_ref,
                 s_scr, cm_scr, m_scr, acc_scr):
    qt = qt_ref[0, 0]
    tq = qt.shape[1]
    n_lat = kl_ref.shape[3]
    m_scr[...] = jnp.full(m_scr.shape, NEG_BIG, f32)
    acc_scr[...] = jnp.zeros(acc_scr.shape, f32)

    n_col = tq // MXU_COLS
    n_chunks = n_lat // TK_ATT

    def after_value(x, v):
        zero = lax.shift_right_logical(
            lax.shift_right_logical(pltpu.bitcast(v, jnp.uint32), jnp.uint32(16)), jnp.uint32(16))
        return pltpu.bitcast(pltpu.bitcast(x, jnp.uint32) | zero, x.dtype)

    def scores(slot, keys, mi, t, after=None):
        tk = keys.shape[0]
        ts = slice(t * MXU_COLS, (t + 1) * MXU_COLS)
        q_tile = qt[:, ts]
        if after is not None:
            q_tile = after_value(q_tile, after)
        st = jnp.dot(keys, q_tile, preferred_element_type=f32)
        for c in range(SLABS_PER_TILE):
            s_scr[slot, mi, t * SLABS_PER_TILE + c, 0:tk, :] = (
                st[:, c * HEAD_DIM:(c + 1) * HEAD_DIM])
        cm_scr[slot, mi, :, ts] = jnp.max(st.reshape(tk // 8, 8, MXU_COLS), axis=0)

    def stats(slot, mi, t):
        ts = slice(t * MXU_COLS, (t + 1) * MXU_COLS)
        m_old = m_scr[mi, :, ts]
        m_new = jnp.maximum(m_old, jnp.max(cm_scr[slot, mi, :, ts], axis=0, keepdims=True))
        m_scr[mi, :, ts] = m_new
        return m_new, jnp.exp2(m_old - m_new)

    def exp_pv(slot, vt, mi, t, m_new, alpha):
        tk = vt.shape[1]
        ts = slice(t * MXU_COLS, (t + 1) * MXU_COLS)
        mb = [jnp.broadcast_to(m_new[:, c * HEAD_DIM:(c + 1) * HEAD_DIM], (MXU_COLS, HEAD_DIM))
              for c in range(SLABS_PER_TILE)]
        pv = None
        for kb in range(0, tk, MXU_COLS):
            rs = slice(kb, kb + MXU_COLS)
            p = jnp.concatenate(
                [jnp.exp2(s_scr[slot, mi, t * SLABS_PER_TILE + c, rs, :] - mb[c]).astype(bf16)
                 for c in range(SLABS_PER_TILE)], axis=1)
            part = jnp.dot(vt[:, rs], p, preferred_element_type=f32)
            pv = part if pv is None else pv + part
        acc_scr[mi, :, ts] = alpha * acc_scr[mi, :, ts] + pv

    def keys_of(j, mi):
        if j < n_chunks:
            return kl_ref[0, 0, mi, j * TK_ATT:(j + 1) * TK_ATT, :]
        return kc_ref[0, 0, mi]

    def values_of(j):
        if j < n_chunks:
            return vtl_ref[0, 0, :, j * TK_ATT:(j + 1) * TK_ATT]
        return vtc_ref[0, 0]

    for mi in range(2):
        for t in range(n_col):
            scores(0, keys_of(0, mi), mi, t)
    for j in range(n_chunks + 1):
        for mi in range(2):
            for t in range(n_col):
                m_new, alpha = stats(j % 2, mi, t)
                if j < n_chunks:
                    scores((j + 1) % 2, keys_of(j + 1, mi), mi, t, after=m_new)
                exp_pv(j % 2, values_of(j), mi, t, m_new, alpha)

    lam_v = lam_ref[...]
    lam = (jnp.exp(jnp.sum(lam_v[0:1] * lam_v[1:2], axis=-1, keepdims=True))
           - jnp.exp(jnp.sum(lam_v[2:3] * lam_v[3:4], axis=-1, keepdims=True))
           + LAMBDA_INIT)
    o1 = acc_scr[0, 0:HEAD_DIM, :] / acc_scr[0, HEAD_DIM:HEAD_DIM + 1, :]
    o2 = acc_scr[1, 0:HEAD_DIM, :] / acc_scr[1, HEAD_DIM:HEAD_DIM + 1, :]
    o = o1 - lam * o2
    ms = jnp.mean(o * o, axis=0, keepdims=True)
    o = o * lax.rsqrt(ms + EPS) * (gsub_ref[...] * (1.0 - LAMBDA_INIT))
    o_ref[0] = o.T.astype(bf16)


def _attn_call(lam_rows, g_sub, qt, k_lat, k_ctx, vt_lat, vt_ctx):
    b, _, _, n = qt.shape
    n_ctx = k_ctx.shape[3]
    tq = TQ_ATT
    return pl.pallas_call(
        _attn_kernel,
        grid=(b, ATT_HEADS, n // tq),
        in_specs=[
            pl.BlockSpec((4, QK_DIM), lambda bb, h, i: (0, 0)),
            pl.BlockSpec((HEAD_DIM, 1), lambda bb, h, i: (0, 0)),
            pl.BlockSpec((1, 1, HEAD_DIM, tq), lambda bb, h, i: (bb, h, 0, i)),
            pl.BlockSpec((1, 1, 2, n, HEAD_DIM), lambda bb, h, i: (bb, h, 0, 0, 0)),
            pl.BlockSpec((1, 1, 2, n_ctx, HEAD_DIM), lambda bb, h, i: (bb, h, 0, 0, 0)),
            pl.BlockSpec((1, 1, VT_ROWS, n), lambda bb, h, i: (bb, h, 0, 0)),
            pl.BlockSpec((1, 1, VT_ROWS, n_ctx), lambda bb, h, i: (bb, h, 0, 0)),
        ],
        out_specs=pl.BlockSpec((1, tq, HEAD_DIM), lambda bb, h, i: (bb, i, h)),
        out_shape=jax.ShapeDtypeStruct((b, n, ATT_WIDTH), bf16),
        scratch_shapes=[
            pltpu.VMEM((2, 2, tq // HEAD_DIM, TK_ATT, HEAD_DIM), f32),
            pltpu.VMEM((2, 2, 8, tq), f32),
            pltpu.VMEM((2, 1, tq), f32),
            pltpu.VMEM((2, VT_ROWS, tq), f32),
        ],
        compiler_params=_cparams(("parallel", "parallel", "arbitrary")),
        name="attn",
    )(lam_rows, g_sub.reshape(HEAD_DIM, 1), qt, k_lat, k_ctx, vt_lat, vt_ctx)


def _mixout_kernel(x_ref, mod_ref, att_ref, u_ref, up_ref, un_ref, wp_ref, ps_ref, wo_ref,
                   o_ref, ubuf, *, n_tok):
    i = pl.program_id(1)
    nt = pl.num_programs(1)
    tm = x_ref.shape[1]
    hl = POOL_HALO
    ubuf[0:hl, :] = jnp.where(i > 0, up_ref[0], 0.0)
    ubuf[hl:hl + tm, :] = u_ref[0]
    ubuf[hl + tm:, :] = jnp.where(i < nt - 1, un_ref[0], 0.0)

    t = i * tm + lax.broadcasted_iota(jnp.int32, (tm, 1), 0)
    pooled = []
    for gi, w in enumerate(POOL_WINDOWS):
        lo = w // 2
        hi = w - lo - 1
        sl = slice(gi * POOL_GROUP_DIM, (gi + 1) * POOL_GROUP_DIM)
        total = ubuf[hl - lo:hl - lo + tm, sl]
        for d in range(-lo + 1, hi + 1):
            total = total + ubuf[hl + d:hl + d + tm, sl]
        cnt = (jnp.minimum(t + hi, n_tok - 1) - jnp.maximum(t - lo, 0) + 1).astype(f32)
        diff = (total / cnt - ubuf[hl:hl + tm, sl]).astype(bf16)
        pooled.append(jnp.dot(diff, wp_ref[gi], preferred_element_type=f32))
    pool = (jnp.concatenate(pooled, axis=-1) * ps_ref[...]).astype(bf16)
    y = jnp.dot(att_ref[0], wo_ref[0:ATT_WIDTH, :], preferred_element_type=f32)
    y = y + jnp.dot(pool, wo_ref[ATT_WIDTH:, :], preferred_element_type=f32)
    o_ref[0] = x_ref[0] + mod_ref[0, 5:6, :] * y


def _mixout_call(x, mod, att, u, w_pool, pool_scale, w_out):
    b, n, d = x.shape
    tm = TM_MIX
    hb = tm // POOL_HALO
    last_halo_block = n // POOL_HALO - 1
    return pl.pallas_call(
        functools.partial(_mixout_kernel, n_tok=n),
        grid=(b, n // tm),
        in_specs=[
            pl.BlockSpec((1, tm, d), lambda bb, i: (bb, i, 0)),
            pl.BlockSpec((1, N_MOD, d), lambda bb, i: (bb, 0, 0)),
            pl.BlockSpec((1, tm, ATT_WIDTH), lambda bb, i: (bb, i, 0)),
            pl.BlockSpec((1, tm, POOL_WIDTH), lambda bb, i: (bb, i, 0)),
            pl.BlockSpec((1, POOL_HALO, POOL_WIDTH),
                         lambda bb, i: (bb, jnp.maximum(i * hb - 1, 0), 0)),
            pl.BlockSpec((1, POOL_HALO, POOL_WIDTH),
                         lambda bb, i: (bb, jnp.minimum((i + 1) * hb, last_halo_block), 0)),
            pl.BlockSpec(w_pool.shape, lambda bb, i: (0, 0, 0)),
            pl.BlockSpec((1, POOL_WIDTH), lambda bb, i: (0, 0)),
            pl.BlockSpec(w_out.shape, lambda bb, i: (0, 0)),
        ],
        out_specs=pl.BlockSpec((1, tm, d), lambda bb, i: (bb, i, 0)),
        out_shape=jax.ShapeDtypeStruct(x.shape, f32),
        scratch_shapes=[pltpu.VMEM((tm + 2 * POOL_HALO, POOL_WIDTH), f32)],
        compiler_params=_cparams(("parallel", "parallel")),
        name="mixout",
    )(x, mod, att, u, u, u, w_pool, pool_scale.reshape(1, POOL_WIDTH), w_out)


def _rope_tables(n):
    rows = n // GRID_W
    row = jnp.repeat(jnp.arange(rows, dtype=f32), GRID_W)
    col = jnp.tile(jnp.arange(GRID_W, dtype=f32), rows)
    nf = QK_DIM // 4
    freqs = ROPE_BASE ** (-jnp.arange(nf, dtype=f32) / nf)
    ar = row[:, None] * freqs
    ac = col[:, None] * freqs
    ang = jnp.concatenate([ar, ar, ac, ac], axis=-1)
    sign = jnp.tile(jnp.concatenate([-jnp.ones((nf,), f32), jnp.ones((nf,), f32)]), 2)
    cos = jnp.tile(jnp.cos(ang), (1, 2))
    sin_signed = jnp.tile(jnp.sin(ang) * sign, (1, 2))
    return cos, sin_signed


def kernel(x, c, ctx, c_ctx, w_mod, b_mod, g_ffn1, ffn1_w_in, ffn1_w_out, g_mix, w_in,
           lambda_q1, lambda_k1, lambda_q2, lambda_k2, g_sub, w_pool, pool_scale, w_out,
           g_ffn2, ffn2_w_in, ffn2_w_out, g_final):
    b, n, d = x.shape
    l = 0
    cond = jnp.zeros((MOD_ROWS, d), f32).at[:b].set(c).at[b].set(c_ctx)
    mod = _mod_call(cond, w_mod[l], b_mod[l]).reshape(MOD_ROWS, N_MOD, d)
    lat_row = lambda bb: bb
    ctx_row = lambda bb: b

    w1_in, w1_out = ffn1_w_in[l].astype(bf16), ffn1_w_out[l].astype(bf16)
    w2_in, w2_out = ffn2_w_in[l].astype(bf16), ffn2_w_out[l].astype(bf16)
    w_in_b = w_in[l].astype(bf16)
    w_o_b = w_out[l].astype(bf16)
    w_pool_b = w_pool[l].astype(bf16)

    x1 = _ffn_call(x, mod, lat_row, 0, g_ffn1[l], w1_in, w1_out)
    cx1 = _ffn_call(ctx, mod, ctx_row, 0, g_ffn1[l], w1_in, w1_out)

    qt, k_lat, vt_lat, u = _inproj_call(x1, mod, lat_row, g_mix[l], w_in_b, _rope_tables(n))
    k_ctx, vt_ctx = _inproj_call(cx1, mod, ctx_row, g_mix[l],
                                 w_in_b[:, ATT_WIDTH:3 * ATT_WIDTH])

    lam_rows = jnp.stack([lambda_q1[l], lambda_k1[l], lambda_q2[l], lambda_k2[l]]).astype(f32)
    att = _attn_call(lam_rows, g_sub[l], qt, k_lat, k_ctx, vt_lat, vt_ctx)

    x2 = _mixout_call(x1, mod, att, u, w_pool_b, pool_scale[l], w_o_b)
    return _ffn_call(x2, mod, lat_row, 6, g_ffn2[l], w2_in, w2_out, g_final)
```

```python
import functools
import math

import jax
import jax.numpy as jnp
from jax import lax
from jax.experimental import pallas as pl
from jax.experimental.pallas import tpu as pltpu

D_MODEL = 1024
GRID_W = 64
ATT_WIDTH = 512
POOL_WIDTH = 512
ATT_HEADS = 4
HEAD_DIM = 128
QK_DIM = 64
POOL_WINDOWS = (2, 4, 8, 16)
POOL_GROUP_DIM = 128
assert POOL_WINDOWS == tuple(2 << g for g in range(len(POOL_WINDOWS)))
D_FF = 2816
N_MOD = 9
ROPE_BASE = 10000.0
EPS = 1e-6
LAMBDA_INIT = 0.8 - 0.6 * math.exp(-0.3 * 0)

MXU_COLS = 256
SLABS_PER_TILE = MXU_COLS // HEAD_DIM
MOD_ROWS = 16
POOL_HALO = 8
NEG_BIG = -1e30
Q_SCALE = (QK_DIM ** -0.5) * math.log2(math.e)

VMEM_LIMIT = 56 * 1024 * 1024

TM_FFN = 1024
SUB_FFN = 512
TM_PROJ = 512
TM_MIX = 512
TQ_ATT = 1024
TK_ATT = 1024
VT_ROWS = HEAD_DIM + 16
FF_CHUNKS = (512, 512, 512, 512, 512, 256)

bf16 = jnp.bfloat16
f32 = jnp.float32


def _cparams(sem, flags=None):
    return pltpu.CompilerParams(dimension_semantics=sem, vmem_limit_bytes=VMEM_LIMIT,
                                flags=flags)


def _zero_after(v):
    bits = pltpu.bitcast(v, jnp.uint32)
    zero = lax.shift_right_logical(lax.shift_right_logical(bits, jnp.uint32(16)), jnp.uint32(16))
    return pltpu.bitcast(zero, f32)


def _norm_mod(x, g, shift, scale):
    ms = jnp.mean(x * x, axis=-1, keepdims=True)
    y = x * lax.rsqrt(ms + EPS) * g
    return y * (1.0 + scale) + shift


def _mod_kernel(cond_ref, w_ref, b_ref, o_ref):
    cnd = cond_ref[...]
    a = (cnd * jax.nn.sigmoid(cnd)).astype(bf16)
    o_ref[...] = jnp.dot(a, w_ref[...].astype(bf16), preferred_element_type=f32) + b_ref[...]


def _mod_call(cond, w_mod, b_mod):
    n_out = w_mod.shape[1]
    tn = D_MODEL
    return pl.pallas_call(
        _mod_kernel,
        grid=(n_out // tn,),
        in_specs=[
            pl.BlockSpec((MOD_ROWS, D_MODEL), lambda j: (0, 0)),
            pl.BlockSpec((D_MODEL, tn), lambda j: (0, j)),
            pl.BlockSpec((1, tn), lambda j: (0, j)),
        ],
        out_specs=pl.BlockSpec((MOD_ROWS, tn), lambda j: (0, j)),
        out_shape=jax.ShapeDtypeStruct((MOD_ROWS, n_out), f32),
        compiler_params=_cparams(("arbitrary",)),
        name="mod",
    )(cond, w_mod, b_mod.reshape(1, n_out))


def _ffn_kernel(x_ref, mod_ref, g_ref, win_ref, wout_ref, *rest, k0, final):
    if final:
        gf_ref, o_ref = rest
    else:
        (o_ref,) = rest
    shift = mod_ref[0, k0:k0 + 1, :]
    scale = mod_ref[0, k0 + 1:k0 + 2, :]
    gate = mod_ref[0, k0 + 2:k0 + 3, :]
    sub = min(SUB_FFN, x_ref.shape[1])
    for r0 in range(0, x_ref.shape[1], sub):
        rows = slice(r0, r0 + sub)
        x = x_ref[0, rows, :]
        h = _norm_mod(x, g_ref[...], shift, scale).astype(bf16)
        acc = None
        off = 0
        for tf in FF_CHUNKS:
            gt = jnp.dot(h, win_ref[:, off:off + tf], preferred_element_type=f32)
            up = jnp.dot(h, win_ref[:, D_FF + off:D_FF + off + tf], preferred_element_type=f32)
            a = (gt * jax.nn.sigmoid(gt) * up).astype(bf16)
            part = jnp.dot(a, wout_ref[off:off + tf, :], preferred_element_type=f32)
            acc = part if acc is None else acc + part
            off += tf
        y = x + 0.5 * gate * acc
        if final:
            ms = jnp.mean(y * y, axis=-1, keepdims=True)
            y = y * lax.rsqrt(ms + EPS) * gf_ref[...]
        o_ref[0, rows, :] = y


def _ffn_call(x, mod, row_of_batch, k0, g, w_in, w_out, g_final=None):
    bx, t, d = x.shape
    tm = min(TM_FFN, t)
    final = g_final is not None
    in_specs = [
        pl.BlockSpec((1, tm, d), lambda b, i: (b, i, 0)),
        pl.BlockSpec((1, N_MOD, d), lambda b, i: (row_of_batch(b), 0, 0)),
        pl.BlockSpec((1, d), lambda b, i: (0, 0)),
        pl.BlockSpec(w_in.shape, lambda b, i: (0, 0), pipeline_mode=pl.Buffered(1)),
        pl.BlockSpec(w_out.shape, lambda b, i: (0, 0), pipeline_mode=pl.Buffered(1)),
    ]
    args = [x, mod, g.reshape(1, d), w_in, w_out]
    if final:
        in_specs.append(pl.BlockSpec((1, d), lambda b, i: (0, 0)))
        args.append(g_final.reshape(1, d))
    return pl.pallas_call(
        functools.partial(_ffn_kernel, k0=k0, final=final),
        grid=(bx, t // tm),
        in_specs=in_specs,
        out_specs=pl.BlockSpec((1, tm, d), lambda b, i: (b, i, 0)),
        out_shape=jax.ShapeDtypeStruct(x.shape, f32),
        compiler_params=_cparams(("parallel", "parallel")),
        name="ffn_final" if final else "ffn",
    )(*args)


def _rope(t, cos, sin_signed, first_half):
    w = t.shape[-1]
    partner = jnp.where(first_half, pltpu.roll(t, w - 16, axis=1), pltpu.roll(t, 16, axis=1))
    return t * cos + partner * sin_signed


def _inproj_kernel(x_ref, mod_ref, g_ref, w_ref, *rest, latent):
    if latent:
        cos_ref, sin_ref, qt_ref, k_ref, vt_ref, u_ref = rest
    else:
        k_ref, vt_ref = rest
    x = x_ref[0]
    shift = mod_ref[0, 3:4, :]
    scale = mod_ref[0, 4:5, :]
    h = _norm_mod(x, g_ref[...], shift, scale).astype(bf16)
    hx = jnp.dot(h, w_ref[...], preferred_element_type=f32)
    tm = x.shape[0]
    lane = lax.broadcasted_iota(jnp.int32, (tm, HEAD_DIM), 1)
    map1 = lane < QK_DIM
    if latent:
        cos = jnp.concatenate([cos_ref[...]] * ATT_HEADS, axis=1)
        sin_signed = jnp.concatenate([sin_ref[...]] * ATT_HEADS, axis=1)
        lane_w = lax.broadcasted_iota(jnp.int32, (tm, ATT_WIDTH), 1)
        first_half = (lane_w % 32) < 16
        q = _rope(hx[:, :ATT_WIDTH], cos, sin_signed, first_half) * Q_SCALE
        k = _rope(hx[:, ATT_WIDTH:2 * ATT_WIDTH], cos, sin_signed, first_half)
        v = hx[:, 2 * ATT_WIDTH:3 * ATT_WIDTH]
        u_ref[0] = hx[:, 3 * ATT_WIDTH:]
    else:
        k = hx[:, :ATT_WIDTH]
        v = hx[:, ATT_WIDTH:]
    pad_row = lax.broadcasted_iota(jnp.int32, (VT_ROWS - HEAD_DIM, tm), 0)
    ones_pad = jnp.where(pad_row == 0, 1.0, 0.0).astype(bf16)
    for hh in range(ATT_HEADS):
        hs = slice(hh * HEAD_DIM, (hh + 1) * HEAD_DIM)
        kh = k[:, hs]
        k_ref[0, hh, 0] = jnp.where(map1, kh, 0.0).astype(bf16)
        k_ref[0, hh, 1] = jnp.where(map1, 0.0, kh).astype(bf16)
        vt_ref[0, hh, 0:HEAD_DIM, :] = v[:, hs].T.astype(bf16)
        vt_ref[0, hh, HEAD_DIM:, :] = ones_pad
        if latent:
            qt_ref[0, hh] = q[:, hs].T.astype(bf16)


def _inproj_call(x, mod, row_of_batch, g, w, rope_tabs=None):
    bx, t, d = x.shape
    tm = min(TM_PROJ, t)
    latent = rope_tabs is not None
    in_specs = [
        pl.BlockSpec((1, tm, d), lambda b, i: (b, i, 0)),
        pl.BlockSpec((1, N_MOD, d), lambda b, i: (row_of_batch(b), 0, 0)),
        pl.BlockSpec((1, d), lambda b, i: (0, 0)),
        pl.BlockSpec(w.shape, lambda b, i: (0, 0), pipeline_mode=pl.Buffered(1)),
    ]
    args = [x, mod, g.reshape(1, d), w]
    k_spec = pl.BlockSpec((1, ATT_HEADS, 2, tm, HEAD_DIM), lambda b, i: (b, 0, 0, i, 0))
    k_shape = jax.ShapeDtypeStruct((bx, ATT_HEADS, 2, t, HEAD_DIM), bf16)
    vt_spec = pl.BlockSpec((1, ATT_HEADS, VT_ROWS, tm), lambda b, i: (b, 0, 0, i))
    vt_shape = jax.ShapeDtypeStruct((bx, ATT_HEADS, VT_ROWS, t), bf16)
    if latent:
        in_specs += [pl.BlockSpec((tm, HEAD_DIM), lambda b, i: (i, 0))] * 2
        args += list(rope_tabs)
        out_specs = [pl.BlockSpec((1, ATT_HEADS, HEAD_DIM, tm), lambda b, i: (b, 0, 0, i)),
                     k_spec, vt_spec,
                     pl.BlockSpec((1, tm, POOL_WIDTH), lambda b, i: (b, i, 0))]
        out_shape = [jax.ShapeDtypeStruct((bx, ATT_HEADS, HEAD_DIM, t), bf16), k_shape, vt_shape,
                     jax.ShapeDtypeStruct((bx, t, POOL_WIDTH), f32)]
    else:
        out_specs = [k_spec, vt_spec]
        out_shape = [k_shape, vt_shape]
    return pl.pallas_call(
        functools.partial(_inproj_kernel, latent=latent),
        grid=(bx, t // tm),
        in_specs=in_specs,
        out_specs=out_specs,
        out_shape=out_shape,
        compiler_params=_cparams(("parallel", "parallel")),
        name="inproj_lat" if latent else "inproj_ctx",
    )(*args)


def _attn_kernel(lam_ref, gsub_ref, qt_ref, kl_ref, kc_ref, vtl_ref, vtc_ref, o_ref,
                 s_scr, cm_scr, m_scr, acc_scr):
    qt = qt_ref[0, 0]
    tq = qt.shape[1]
    n_lat = kl_ref.shape[3]
    m_scr[...] = jnp.full(m_scr.shape, NEG_BIG, f32)
    acc_scr[...] = jnp.zeros(acc_scr.shape, f32)

    n_col = tq // MXU_COLS
    n_chunks = n_lat // TK_ATT

    def scores(slot, keys, mi, t, after=None):
        tk = keys.shape[0]
        ts = slice(t * MXU_COLS, (t + 1) * MXU_COLS)
        q_tile = qt[:, ts]
        if after is not None:
            q_tile = q_tile + _zero_after(after).astype(bf16)
        st = jnp.dot(keys, q_tile, preferred_element_type=f32)
        for c in range(SLABS_PER_TILE):
            s_scr[slot, mi, t * SLABS_PER_TILE + c, 0:tk, :] = (
                st[:, c * HEAD_DIM:(c + 1) * HEAD_DIM])
        cm_scr[slot, mi, :, ts] = jnp.max(st.reshape(tk // 8, 8, MXU_COLS), axis=0)

    def stats(slot, mi, t):
        ts = slice(t * MXU_COLS, (t + 1) * MXU_COLS)
        m_old = m_scr[mi, :, ts]
        m_new = jnp.maximum(m_old, jnp.max(cm_scr[slot, mi, :, ts], axis=0, keepdims=True))
        m_scr[mi, :, ts] = m_new
        return m_new, jnp.exp2(m_old - m_new)

    def exp_pv(slot, vt, mi, t, m_new, alpha):
        tk = vt.shape[1]
        ts = slice(t * MXU_COLS, (t + 1) * MXU_COLS)
        mb = [jnp.broadcast_to(m_new[:, c * HEAD_DIM:(c + 1) * HEAD_DIM], (MXU_COLS, HEAD_DIM))
              for c in range(SLABS_PER_TILE)]
        pv = None
        for kb in range(0, tk, MXU_COLS):
            rs = slice(kb, kb + MXU_COLS)
            p = jnp.concatenate(
                [jnp.exp2(s_scr[slot, mi, t * SLABS_PER_TILE + c, rs, :] - mb[c]).astype(bf16)
                 for c in range(SLABS_PER_TILE)], axis=1)
            part = jnp.dot(vt[:, rs], p, preferred_element_type=f32)
            pv = part if pv is None else pv + part
        acc_scr[mi, :, ts] = alpha * acc_scr[mi, :, ts] + pv

    def keys_of(j, mi):
        if j < n_chunks:
            return kl_ref[0, 0, mi, j * TK_ATT:(j + 1) * TK_ATT, :]
        return kc_ref[0, 0, mi]

    def values_of(j):
        if j < n_chunks:
            return vtl_ref[0, 0, :, j * TK_ATT:(j + 1) * TK_ATT]
        return vtc_ref[0, 0]

    for mi in range(2):
        for t in range(n_col):
            scores(0, keys_of(0, mi), mi, t)
    for j in range(n_chunks + 1):
        for mi in range(2):
            for t in range(n_col):
                m_new, alpha = stats(j % 2, mi, t)
                if j < n_chunks:
                    scores((j + 1) % 2, keys_of(j + 1, mi), mi, t, after=m_new)
                exp_pv(j % 2, values_of(j), mi, t, m_new, alpha)

    lam_v = lam_ref[...]
    lam = (jnp.exp(jnp.sum(lam_v[0:1] * lam_v[1:2], axis=-1, keepdims=True))
           - jnp.exp(jnp.sum(lam_v[2:3] * lam_v[3:4], axis=-1, keepdims=True))
           + LAMBDA_INIT)
    o1 = acc_scr[0, 0:HEAD_DIM, :] / acc_scr[0, HEAD_DIM:HEAD_DIM + 1, :]
    o2 = acc_scr[1, 0:HEAD_DIM, :] / acc_scr[1, HEAD_DIM:HEAD_DIM + 1, :]
    o = o1 - lam * o2
    ms = jnp.mean(o * o, axis=0, keepdims=True)
    o = o * lax.rsqrt(ms + EPS) * (gsub_ref[...] * (1.0 - LAMBDA_INIT))
    o_ref[0] = o.T.astype(bf16)


def _attn_call(lam_rows, g_sub, qt, k_lat, k_ctx, vt_lat, vt_ctx):
    b, _, _, n = qt.shape
    n_ctx = k_ctx.shape[3]
    tq = TQ_ATT
    return pl.pallas_call(
        _attn_kernel,
        grid=(b, ATT_HEADS, n // tq),
        in_specs=[
            pl.BlockSpec((4, QK_DIM), lambda bb, h, i: (0, 0)),
            pl.BlockSpec((HEAD_DIM, 1), lambda bb, h, i: (0, 0)),
            pl.BlockSpec((1, 1, HEAD_DIM, tq), lambda bb, h, i: (bb, h, 0, i)),
            pl.BlockSpec((1, 1, 2, n, HEAD_DIM), lambda bb, h, i: (bb, h, 0, 0, 0)),
            pl.BlockSpec((1, 1, 2, n_ctx, HEAD_DIM), lambda bb, h, i: (bb, h, 0, 0, 0)),
            pl.BlockSpec((1, 1, VT_ROWS, n), lambda bb, h, i: (bb, h, 0, 0)),
            pl.BlockSpec((1, 1, VT_ROWS, n_ctx), lambda bb, h, i: (bb, h, 0, 0)),
        ],
        out_specs=pl.BlockSpec((1, tq, HEAD_DIM), lambda bb, h, i: (bb, i, h)),
        out_shape=jax.ShapeDtypeStruct((b, n, ATT_WIDTH), bf16),
        scratch_shapes=[
            pltpu.VMEM((2, 2, tq // HEAD_DIM, TK_ATT, HEAD_DIM), f32),
            pltpu.VMEM((2, 2, 8, tq), f32),
            pltpu.VMEM((2, 1, tq), f32),
            pltpu.VMEM((2, VT_ROWS, tq), f32),
        ],
        compiler_params=_cparams(("parallel", "parallel", "arbitrary")),
        name="attn",
    )(lam_rows, g_sub.reshape(HEAD_DIM, 1), qt, k_lat, k_ctx, vt_lat, vt_ctx)


def _mixout_kernel(x_ref, mod_ref, att_ref, u_ref, up_ref, un_ref, wp_ref, ps_ref, wo_ref,
                   o_ref, run, *, n_tok):
    i = pl.program_id(1)
    nt = pl.num_programs(1)
    tm = x_ref.shape[1]
    hl = POOL_HALO
    y_att = jnp.dot(att_ref[0], wo_ref[0:ATT_WIDTH, :], preferred_element_type=f32)
    base = 2 * hl
    top = base + tm + hl
    zeros = jnp.zeros((hl, POOL_WIDTH), f32)
    run[0, 0:hl, :] = zeros
    run[0, hl:base, :] = jnp.where(i > 0, up_ref[0], 0.0)
    run[0, base:base + tm, :] = u_ref[0]
    run[0, base + tm:top, :] = jnp.where(i < nt - 1, un_ref[0], 0.0)
    for k in range(1, len(POOL_WINDOWS)):
        reach = 1 << (k - 1)
        lanes = slice((k - 1) * POOL_GROUP_DIM, POOL_WIDTH)
        run[k, 0:hl, :] = zeros
        run[k, hl:top, lanes] = run[k - 1, hl - reach:top - reach, lanes] + run[k - 1, hl:top, lanes]

    t_head = i * tm + lax.broadcasted_iota(jnp.int32, (hl, 1), 0)
    t_tail = t_head + (tm - hl)
    pooled = []
    for gi, w in enumerate(POOL_WINDOWS):
        lo = w // 2
        hi = w - lo - 1
        sl = slice(gi * POOL_GROUP_DIM, (gi + 1) * POOL_GROUP_DIM)
        if gi + 1 < len(POOL_WINDOWS):
            total = run[gi + 1, base + hi:base + hi + tm, sl]
        else:
            half = w // 2
            total = (run[gi, base + hi - half:base + hi - half + tm, sl]
                     + run[gi, base + hi:base + hi + tm, sl])

        def cnt(t):
            return (jnp.minimum(t + hi, n_tok - 1) - jnp.maximum(t - lo, 0) + 1).astype(f32)

        mean = jnp.concatenate([total[0:hl] / cnt(t_head),
                                total[hl:tm - hl] * (1.0 / w),
                                total[tm - hl:tm] / cnt(t_tail)], axis=0)
        diff = (mean - run[0, base:base + tm, sl]).astype(bf16)
        pooled.append(jnp.dot(diff, wp_ref[gi], preferred_element_type=f32))
    pool = (jnp.concatenate(pooled, axis=-1) * ps_ref[...]).astype(bf16)
    y = y_att + jnp.dot(pool, wo_ref[ATT_WIDTH:, :], preferred_element_type=f32)
    o_ref[0] = x_ref[0] + mod_ref[0, 5:6, :] * y


def _mixout_call(x, mod, att, u, w_pool, pool_scale, w_out):
    b, n, d = x.shape
    tm = TM_MIX
    hb = tm // POOL_HALO
    last_halo_block = n // POOL_HALO - 1
    return pl.pallas_call(
        functools.partial(_mixout_kernel, n_tok=n),
        grid=(b, n // tm),
        in_specs=[
            pl.BlockSpec((1, tm, d), lambda bb, i: (bb, i, 0)),
            pl.BlockSpec((1, N_MOD, d), lambda bb, i: (bb, 0, 0)),
            pl.BlockSpec((1, tm, ATT_WIDTH), lambda bb, i: (bb, i, 0)),
            pl.BlockSpec((1, tm, POOL_WIDTH), lambda bb, i: (bb, i, 0)),
            pl.BlockSpec((1, POOL_HALO, POOL_WIDTH),
                         lambda bb, i: (bb, jnp.maximum(i * hb - 1, 0), 0)),
            pl.BlockSpec((1, POOL_HALO, POOL_WIDTH),
                         lambda bb, i: (bb, jnp.minimum((i + 1) * hb, last_halo_block), 0)),
            pl.BlockSpec(w_pool.shape, lambda bb, i: (0, 0, 0)),
            pl.BlockSpec((1, POOL_WIDTH), lambda bb, i: (0, 0)),
            pl.BlockSpec(w_out.shape, lambda bb, i: (0, 0)),
        ],
        out_specs=pl.BlockSpec((1, tm, d), lambda bb, i: (bb, i, 0)),
        out_shape=jax.ShapeDtypeStruct(x.shape, f32),
        scratch_shapes=[pltpu.VMEM((len(POOL_WINDOWS), tm + 3 * POOL_HALO, POOL_WIDTH), f32)],
        compiler_params=_cparams(("parallel", "parallel")),
        name="mixout",
    )(x, mod, att, u, u, u, w_pool, pool_scale.reshape(1, POOL_WIDTH), w_out)


def _rope_tables(n):
    rows = n // GRID_W
    row = jnp.repeat(jnp.arange(rows, dtype=f32), GRID_W)
    col = jnp.tile(jnp.arange(GRID_W, dtype=f32), rows)
    nf = QK_DIM // 4
    freqs = ROPE_BASE ** (-jnp.arange(nf, dtype=f32) / nf)
    ar = row[:, None] * freqs
    ac = col[:, None] * freqs
    ang = jnp.concatenate([ar, ar, ac, ac], axis=-1)
    sign = jnp.tile(jnp.concatenate([-jnp.ones((nf,), f32), jnp.ones((nf,), f32)]), 2)
    cos = jnp.tile(jnp.cos(ang), (1, 2))
    sin_signed = jnp.tile(jnp.sin(ang) * sign, (1, 2))
    return cos, sin_signed


def kernel(x, c, ctx, c_ctx, w_mod, b_mod, g_ffn1, ffn1_w_in, ffn1_w_out, g_mix, w_in,
           lambda_q1, lambda_k1, lambda_q2, lambda_k2, g_sub, w_pool, pool_scale, w_out,
           g_ffn2, ffn2_w_in, ffn2_w_out, g_final):
    b, n, d = x.shape
    l = 0
    cond = jnp.zeros((MOD_ROWS, d), f32).at[:b].set(c).at[b].set(c_ctx)
    mod = _mod_call(cond, w_mod[l], b_mod[l]).reshape(MOD_ROWS, N_MOD, d)
    lat_row = lambda bb: bb
    ctx_row = lambda bb: b

    w1_in, w1_out = ffn1_w_in[l].astype(bf16), ffn1_w_out[l].astype(bf16)
    w2_in, w2_out = ffn2_w_in[l].astype(bf16), ffn2_w_out[l].astype(bf16)
    w_in_b = w_in[l].astype(bf16)
    w_o_b = w_out[l].astype(bf16)
    w_pool_b = w_pool[l].astype(bf16)

    x1 = _ffn_call(x, mod, lat_row, 0, g_ffn1[l], w1_in, w1_out)
    cx1 = _ffn_call(ctx, mod, ctx_row, 0, g_ffn1[l], w1_in, w1_out)

    qt, k_lat, vt_lat, u = _inproj_call(x1, mod, lat_row, g_mix[l], w_in_b, _rope_tables(n))
    k_ctx, vt_ctx = _inproj_call(cx1, mod, ctx_row, g_mix[l],
                                 w_in_b[:, ATT_WIDTH:3 * ATT_WIDTH])

    lam_rows = jnp.stack([lambda_q1[l], lambda_k1[l], lambda_q2[l], lambda_k2[l]]).astype(f32)
    att = _attn_call(lam_rows, g_sub[l], qt, k_lat, k_ctx, vt_lat, vt_ctx)

    x2 = _mixout_call(x1, mod, att, u, w_pool_b, pool_scale[l], w_o_b)
    return _ffn_call(x2, mod, lat_row, 6, g_ffn2[l], w2_in, w2_out, g_final)
```

```python
import functools
import math

import jax
import jax.numpy as jnp
from jax import lax
from jax.experimental import pallas as pl
from jax.experimental.pallas import tpu as pltpu

D_MODEL = 1024
GRID_W = 64
ATT_WIDTH = 512
POOL_WIDTH = 512
ATT_HEADS = 4
HEAD_DIM = 128
QK_DIM = 64
POOL_WINDOWS = (2, 4, 8, 16)
POOL_GROUP_DIM = 128
assert POOL_WINDOWS == tuple(2 << g for g in range(len(POOL_WINDOWS)))
D_FF = 2816
N_MOD = 9
ROPE_BASE = 10000.0
EPS = 1e-6
LAMBDA_INIT = 0.8 - 0.6 * math.exp(-0.3 * 0)

MXU_COLS = 256
SLABS_PER_TILE = MXU_COLS // HEAD_DIM
MOD_ROWS = 16
POOL_HALO = 8
NEG_BIG = -1e30
Q_SCALE = (QK_DIM ** -0.5) * math.log2(math.e)

VMEM_LIMIT = 56 * 1024 * 1024

TM_FFN = 1024
SUB_FFN = 512
TM_PROJ = 512
TM_MIX = 512
TQ_ATT = 1024
TK_ATT = 1024
TIE_BLOCK = 2
QT_ATT = 2
VT_ROWS = HEAD_DIM + 16
FF_CHUNKS = (512, 512, 512, 512, 512, 256)

bf16 = jnp.bfloat16
f32 = jnp.float32


def _cparams(sem, flags=None):
    return pltpu.CompilerParams(dimension_semantics=sem, vmem_limit_bytes=VMEM_LIMIT,
                                flags=flags)


def _zero_after(v):
    bits = pltpu.bitcast(v, jnp.uint32)
    zero = lax.shift_right_logical(lax.shift_right_logical(bits, jnp.uint32(16)), jnp.uint32(16))
    return pltpu.bitcast(zero, f32)


def _norm_mod(x, g, shift, scale):
    ms = jnp.mean(x * x, axis=-1, keepdims=True)
    y = x * lax.rsqrt(ms + EPS) * g
    return y * (1.0 + scale) + shift


def _mod_kernel(cond_ref, w_ref, b_ref, o_ref):
    cnd = cond_ref[...]
    a = (cnd * jax.nn.sigmoid(cnd)).astype(bf16)
    o_ref[...] = jnp.dot(a, w_ref[...].astype(bf16), preferred_element_type=f32) + b_ref[...]


def _mod_call(cond, w_mod, b_mod):
    n_out = w_mod.shape[1]
    tn = D_MODEL
    return pl.pallas_call(
        _mod_kernel,
        grid=(n_out // tn,),
        in_specs=[
            pl.BlockSpec((MOD_ROWS, D_MODEL), lambda j: (0, 0)),
            pl.BlockSpec((D_MODEL, tn), lambda j: (0, j)),
            pl.BlockSpec((1, tn), lambda j: (0, j)),
        ],
        out_specs=pl.BlockSpec((MOD_ROWS, tn), lambda j: (0, j)),
        out_shape=jax.ShapeDtypeStruct((MOD_ROWS, n_out), f32),
        compiler_params=_cparams(("arbitrary",)),
        name="mod",
    )(cond, w_mod, b_mod.reshape(1, n_out))


def _ffn_kernel(x_ref, mod_ref, g_ref, win_ref, wout_ref, *rest, k0, final):
    if final:
        gf_ref, o_ref = rest
    else:
        (o_ref,) = rest
    shift = mod_ref[0, k0:k0 + 1, :]
    scale = mod_ref[0, k0 + 1:k0 + 2, :]
    gate = mod_ref[0, k0 + 2:k0 + 3, :]
    sub = min(SUB_FFN, x_ref.shape[1])
    for r0 in range(0, x_ref.shape[1], sub):
        rows = slice(r0, r0 + sub)
        x = x_ref[0, rows, :]
        h = _norm_mod(x, g_ref[...], shift, scale).astype(bf16)
        acc = None
        off = 0
        for tf in FF_CHUNKS:
            gt = jnp.dot(h, win_ref[:, off:off + tf], preferred_element_type=f32)
            up = jnp.dot(h, win_ref[:, D_FF + off:D_FF + off + tf], preferred_element_type=f32)
            a = (gt * jax.nn.sigmoid(gt) * up).astype(bf16)
            part = jnp.dot(a, wout_ref[off:off + tf, :], preferred_element_type=f32)
            acc = part if acc is None else acc + part
            off += tf
        y = x + 0.5 * gate * acc
        if final:
            ms = jnp.mean(y * y, axis=-1, keepdims=True)
            y = y * lax.rsqrt(ms + EPS) * gf_ref[...]
        o_ref[0, rows, :] = y


def _ffn_call(x, mod, row_of_batch, k0, g, w_in, w_out, g_final=None):
    bx, t, d = x.shape
    tm = min(TM_FFN, t)
    final = g_final is not None
    in_specs = [
        pl.BlockSpec((1, tm, d), lambda b, i: (b, i, 0)),
        pl.BlockSpec((1, N_MOD, d), lambda b, i: (row_of_batch(b), 0, 0)),
        pl.BlockSpec((1, d), lambda b, i: (0, 0)),
        pl.BlockSpec(w_in.shape, lambda b, i: (0, 0), pipeline_mode=pl.Buffered(1)),
        pl.BlockSpec(w_out.shape, lambda b, i: (0, 0), pipeline_mode=pl.Buffered(1)),
    ]
    args = [x, mod, g.reshape(1, d), w_in, w_out]
    if final:
        in_specs.append(pl.BlockSpec((1, d), lambda b, i: (0, 0)))
        args.append(g_final.reshape(1, d))
    return pl.pallas_call(
        functools.partial(_ffn_kernel, k0=k0, final=final),
        grid=(bx, t // tm),
        in_specs=in_specs,
        out_specs=pl.BlockSpec((1, tm, d), lambda b, i: (b, i, 0)),
        out_shape=jax.ShapeDtypeStruct(x.shape, f32),
        compiler_params=_cparams(("parallel", "parallel")),
        name="ffn_final" if final else "ffn",
    )(*args)


def _rope(t, cos, sin_signed, first_half):
    w = t.shape[-1]
    partner = jnp.where(first_half, pltpu.roll(t, w - 16, axis=1), pltpu.roll(t, 16, axis=1))
    return t * cos + partner * sin_signed


def _inproj_kernel(x_ref, mod_ref, g_ref, w_ref, *rest, latent):
    if latent:
        cos_ref, sin_ref, qt_ref, k_ref, vt_ref, u_ref = rest
    else:
        k_ref, vt_ref = rest
    x = x_ref[0]
    shift = mod_ref[0, 3:4, :]
    scale = mod_ref[0, 4:5, :]
    h = _norm_mod(x, g_ref[...], shift, scale).astype(bf16)
    hx = jnp.dot(h, w_ref[...], preferred_element_type=f32)
    tm = x.shape[0]
    lane = lax.broadcasted_iota(jnp.int32, (tm, HEAD_DIM), 1)
    map1 = lane < QK_DIM
    if latent:
        cos = jnp.concatenate([cos_ref[...]] * ATT_HEADS, axis=1)
        sin_signed = jnp.concatenate([sin_ref[...]] * ATT_HEADS, axis=1)
        lane_w = lax.broadcasted_iota(jnp.int32, (tm, ATT_WIDTH), 1)
        first_half = (lane_w % 32) < 16
        q = _rope(hx[:, :ATT_WIDTH], cos, sin_signed, first_half) * Q_SCALE
        k = _rope(hx[:, ATT_WIDTH:2 * ATT_WIDTH], cos, sin_signed, first_half)
        v = hx[:, 2 * ATT_WIDTH:3 * ATT_WIDTH]
        u_ref[0] = hx[:, 3 * ATT_WIDTH:]
    else:
        k = hx[:, :ATT_WIDTH]
        v = hx[:, ATT_WIDTH:]
    pad_row = lax.broadcasted_iota(jnp.int32, (VT_ROWS - HEAD_DIM, tm), 0)
    ones_pad = jnp.where(pad_row == 0, 1.0, 0.0).astype(bf16)
    for hh in range(ATT_HEADS):
        hs = slice(hh * HEAD_DIM, (hh + 1) * HEAD_DIM)
        kh = k[:, hs]
        k_ref[0, hh, 0] = jnp.where(map1, kh, 0.0).astype(bf16)
        k_ref[0, hh, 1] = jnp.where(map1, 0.0, kh).astype(bf16)
        vt_ref[0, hh, 0:HEAD_DIM, :] = v[:, hs].T.astype(bf16)
        vt_ref[0, hh, HEAD_DIM:, :] = ones_pad
        if latent:
            qt_ref[0, hh] = q[:, hs].T.astype(bf16)


def _inproj_call(x, mod, row_of_batch, g, w, rope_tabs=None):
    bx, t, d = x.shape
    tm = min(TM_PROJ, t)
    latent = rope_tabs is not None
    in_specs = [
        pl.BlockSpec((1, tm, d), lambda b, i: (b, i, 0)),
        pl.BlockSpec((1, N_MOD, d), lambda b, i: (row_of_batch(b), 0, 0)),
        pl.BlockSpec((1, d), lambda b, i: (0, 0)),
        pl.BlockSpec(w.shape, lambda b, i: (0, 0), pipeline_mode=pl.Buffered(1)),
    ]
    args = [x, mod, g.reshape(1, d), w]
    k_spec = pl.BlockSpec((1, ATT_HEADS, 2, tm, HEAD_DIM), lambda b, i: (b, 0, 0, i, 0))
    k_shape = jax.ShapeDtypeStruct((bx, ATT_HEADS, 2, t, HEAD_DIM), bf16)
    vt_spec = pl.BlockSpec((1, ATT_HEADS, VT_ROWS, tm), lambda b, i: (b, 0, 0, i))
    vt_shape = jax.ShapeDtypeStruct((bx, ATT_HEADS, VT_ROWS, t), bf16)
    if latent:
        in_specs += [pl.BlockSpec((tm, HEAD_DIM), lambda b, i: (i, 0))] * 2
        args += list(rope_tabs)
        out_specs = [pl.BlockSpec((1, ATT_HEADS, HEAD_DIM, tm), lambda b, i: (b, 0, 0, i)),
                     k_spec, vt_spec,
                     pl.BlockSpec((1, tm, POOL_WIDTH), lambda b, i: (b, i, 0))]
        out_shape = [jax.ShapeDtypeStruct((bx, ATT_HEADS, HEAD_DIM, t), bf16), k_shape, vt_shape,
                     jax.ShapeDtypeStruct((bx, t, POOL_WIDTH), f32)]
    else:
        out_specs = [k_spec, vt_spec]
        out_shape = [k_shape, vt_shape]
    return pl.pallas_call(
        functools.partial(_inproj_kernel, latent=latent),
        grid=(bx, t // tm),
        in_specs=in_specs,
        out_specs=out_specs,
        out_shape=out_shape,
        compiler_params=_cparams(("parallel", "parallel")),
        name="inproj_lat" if latent else "inproj_ctx",
    )(*args)


def _attn_kernel(lam_ref, gsub_ref, qt_ref, kl_ref, kc_ref, vtl_ref, vtc_ref, o_ref,
                 s_scr, cm_scr, m_scr, acc_scr):
    tq = s_scr.shape[2] * HEAD_DIM
    n_lat = kl_ref.shape[3]
    m_scr[...] = jnp.full(m_scr.shape, NEG_BIG, f32)
    acc_scr[...] = jnp.zeros(acc_scr.shape, f32)

    n_col = tq // MXU_COLS
    n_chunks = n_lat // TK_ATT
    stages = [(ti, j) for ti in range(QT_ATT) for j in range(n_chunks + 1)]

    def scores(s, mi, t, after=None):
        ti, j = stages[s]
        keys = keys_of(j, mi)
        tk = keys.shape[0]
        q_tile = qt_ref[0, 0, :, ti * tq + t * MXU_COLS:ti * tq + (t + 1) * MXU_COLS]
        if after is not None:
            q_tile = q_tile + _zero_after(after).astype(bf16)
        st = jnp.dot(keys, q_tile, preferred_element_type=f32)
        for c in range(SLABS_PER_TILE):
            s_scr[s % 2, mi, t * SLABS_PER_TILE + c, 0:tk, :] = (
                st[:, c * HEAD_DIM:(c + 1) * HEAD_DIM])
        cm_scr[s % 2, mi, :, t * MXU_COLS:(t + 1) * MXU_COLS] = (
            jnp.max(st.reshape(tk // 8, 8, MXU_COLS), axis=0))

    def softmax_pv(s, mi, t):
        ti, j = stages[s]
        slot = s % 2
        vt = values_of(j)
        tk = vt.shape[1]
        ts = slice(t * MXU_COLS, (t + 1) * MXU_COLS)
        m_old = m_scr[ti, mi, :, ts]
        m_new = jnp.maximum(m_old, jnp.max(cm_scr[slot, mi, :, ts], axis=0, keepdims=True))
        m_scr[ti, mi, :, ts] = m_new
        alpha = jnp.exp2(m_old - m_new)
        mb = [jnp.broadcast_to(m_new[:, c * HEAD_DIM:(c + 1) * HEAD_DIM], (MXU_COLS, HEAD_DIM))
              for c in range(SLABS_PER_TILE)]
        pv = None
        done = None
        n_blocks = tk // MXU_COLS
        for bi in range(n_blocks):
            rs = slice(bi * MXU_COLS, (bi + 1) * MXU_COLS)
            e = [jnp.exp2(s_scr[slot, mi, t * SLABS_PER_TILE + c, rs, :] - mb[c])
                 for c in range(SLABS_PER_TILE)]
            p = jnp.concatenate([ec.astype(bf16) for ec in e], axis=1)
            part = jnp.dot(vt[:, rs], p, preferred_element_type=f32)
            pv = part if pv is None else pv + part
            if bi == min(TIE_BLOCK, n_blocks - 1):
                done = jnp.concatenate([jnp.max(ec, axis=0, keepdims=True) for ec in e], axis=1)
        acc_scr[ti, mi, :, ts] = alpha * acc_scr[ti, mi, :, ts] + pv
        return done

    def keys_of(j, mi):
        if j < n_chunks:
            return kl_ref[0, 0, mi, j * TK_ATT:(j + 1) * TK_ATT, :]
        return kc_ref[0, 0, mi]

    def values_of(j):
        if j < n_chunks:
            return vtl_ref[0, 0, :, j * TK_ATT:(j + 1) * TK_ATT]
        return vtc_ref[0, 0]

    def finalize(ti):
        lam_v = lam_ref[...]
        lam = (jnp.exp(jnp.sum(lam_v[0:1] * lam_v[1:2], axis=-1, keepdims=True))
               - jnp.exp(jnp.sum(lam_v[2:3] * lam_v[3:4], axis=-1, keepdims=True))
               + LAMBDA_INIT)
        o1 = acc_scr[ti, 0, 0:HEAD_DIM, :] / acc_scr[ti, 0, HEAD_DIM:HEAD_DIM + 1, :]
        o2 = acc_scr[ti, 1, 0:HEAD_DIM, :] / acc_scr[ti, 1, HEAD_DIM:HEAD_DIM + 1, :]
        o = o1 - lam * o2
        ms = jnp.mean(o * o, axis=0, keepdims=True)
        o = o * lax.rsqrt(ms + EPS) * (gsub_ref[...] * (1.0 - LAMBDA_INIT))
        o_ref[0, ti * tq:(ti + 1) * tq, :] = o.T.astype(bf16)

    for mi in range(2):
        for t in range(n_col):
            scores(0, mi, t)
    prev_done = None
    for s in range(len(stages)):
        for mi in range(2):
            for t in range(n_col):
                if s + 1 < len(stages):
                    scores(s + 1, mi, t, after=prev_done)
                prev_done = softmax_pv(s, mi, t)
        if stages[s][1] == n_chunks:
            finalize(stages[s][0])


def _attn_call(lam_rows, g_sub, qt, k_lat, k_ctx, vt_lat, vt_ctx):
    b, _, _, n = qt.shape
    n_ctx = k_ctx.shape[3]
    tq = TQ_ATT
    tstep = QT_ATT * tq
    return pl.pallas_call(
        _attn_kernel,
        grid=(b, ATT_HEADS, n // tstep),
        in_specs=[
            pl.BlockSpec((4, QK_DIM), lambda bb, h, i: (0, 0)),
            pl.BlockSpec((HEAD_DIM, 1), lambda bb, h, i: (0, 0)),
            pl.BlockSpec((1, 1, HEAD_DIM, tstep), lambda bb, h, i: (bb, h, 0, i)),
            pl.BlockSpec((1, 1, 2, n, HEAD_DIM), lambda bb, h, i: (bb, h, 0, 0, 0)),
            pl.BlockSpec((1, 1, 2, n_ctx, HEAD_DIM), lambda bb, h, i: (bb, h, 0, 0, 0)),
            pl.BlockSpec((1, 1, VT_ROWS, n), lambda bb, h, i: (bb, h, 0, 0)),
            pl.BlockSpec((1, 1, VT_ROWS, n_ctx), lambda bb, h, i: (bb, h, 0, 0)),
        ],
        out_specs=pl.BlockSpec((1, tstep, HEAD_DIM), lambda bb, h, i: (bb, i, h)),
        out_shape=jax.ShapeDtypeStruct((b, n, ATT_WIDTH), bf16),
        scratch_shapes=[
            pltpu.VMEM((2, 2, tq // HEAD_DIM, TK_ATT, HEAD_DIM), f32),
            pltpu.VMEM((2, 2, 8, tq), f32),
            pltpu.VMEM((QT_ATT, 2, 1, tq), f32),
            pltpu.VMEM((QT_ATT, 2, VT_ROWS, tq), f32),
        ],
        compiler_params=_cparams(("parallel", "parallel", "arbitrary")),
        name="attn",
    )(lam_rows, g_sub.reshape(HEAD_DIM, 1), qt, k_lat, k_ctx, vt_lat, vt_ctx)


def _mixout_kernel(x_ref, mod_ref, att_ref, u_ref, up_ref, un_ref, wp_ref, ps_ref, wo_ref,
                   o_ref, run, *, n_tok):
    i = pl.program_id(1)
    nt = pl.num_programs(1)
    tm = x_ref.shape[1]
    hl = POOL_HALO
    y_att = jnp.dot(att_ref[0], wo_ref[0:ATT_WIDTH, :], preferred_element_type=f32)
    base = 2 * hl
    top = base + tm + hl
    zeros = jnp.zeros((hl, POOL_WIDTH), f32)
    run[0, 0:hl, :] = zeros
    run[0, hl:base, :] = jnp.where(i > 0, up_ref[0], 0.0)
    run[0, base:base + tm, :] = u_ref[0]
    run[0, base + tm:top, :] = jnp.where(i < nt - 1, un_ref[0], 0.0)
    for k in range(1, len(POOL_WINDOWS)):
        reach = 1 << (k - 1)
        lanes = slice((k - 1) * POOL_GROUP_DIM, POOL_WIDTH)
        run[k, 0:hl, :] = zeros
        run[k, hl:top, lanes] = run[k - 1, hl - reach:top - reach, lanes] + run[k - 1, hl:top, lanes]

    t_head = i * tm + lax.broadcasted_iota(jnp.int32, (hl, 1), 0)
    t_tail = t_head + (tm - hl)
    pooled = []
    for gi, w in enumerate(POOL_WINDOWS):
        lo = w // 2
        hi = w - lo - 1
        sl = slice(gi * POOL_GROUP_DIM, (gi + 1) * POOL_GROUP_DIM)
        if gi + 1 < len(POOL_WINDOWS):
            total = run[gi + 1, base + hi:base + hi + tm, sl]
        else:
            half = w // 2
            total = (run[gi, base + hi - half:base + hi - half + tm, sl]
                     + run[gi, base + hi:base + hi + tm, sl])

        def cnt(t):
            return (jnp.minimum(t + hi, n_tok - 1) - jnp.maximum(t - lo, 0) + 1).astype(f32)

        mean = jnp.concatenate([total[0:hl] / cnt(t_head),
                                total[hl:tm - hl] * (1.0 / w),
                                total[tm - hl:tm] / cnt(t_tail)], axis=0)
        diff = (mean - run[0, base:base + tm, sl]).astype(bf16)
        pooled.append(jnp.dot(diff, wp_ref[gi], preferred_element_type=f32))
    pool = (jnp.concatenate(pooled, axis=-1) * ps_ref[...]).astype(bf16)
    y = y_att + jnp.dot(pool, wo_ref[ATT_WIDTH:, :], preferred_element_type=f32)
    o_ref[0] = x_ref[0] + mod_ref[0, 5:6, :] * y


def _mixout_call(x, mod, att, u, w_pool, pool_scale, w_out):
    b, n, d = x.shape
    tm = TM_MIX
    hb = tm // POOL_HALO
    last_halo_block = n // POOL_HALO - 1
    return pl.pallas_call(
        functools.partial(_mixout_kernel, n_tok=n),
        grid=(b, n // tm),
        in_specs=[
            pl.BlockSpec((1, tm, d), lambda bb, i: (bb, i, 0)),
            pl.BlockSpec((1, N_MOD, d), lambda bb, i: (bb, 0, 0)),
            pl.BlockSpec((1, tm, ATT_WIDTH), lambda bb, i: (bb, i, 0)),
            pl.BlockSpec((1, tm, POOL_WIDTH), lambda bb, i: (bb, i, 0)),
            pl.BlockSpec((1, POOL_HALO, POOL_WIDTH),
                         lambda bb, i: (bb, jnp.maximum(i * hb - 1, 0), 0)),
            pl.BlockSpec((1, POOL_HALO, POOL_WIDTH),
                         lambda bb, i: (bb, jnp.minimum((i + 1) * hb, last_halo_block), 0)),
            pl.BlockSpec(w_pool.shape, lambda bb, i: (0, 0, 0)),
            pl.BlockSpec((1, POOL_WIDTH), lambda bb, i: (0, 0)),
            pl.BlockSpec(w_out.shape, lambda bb, i: (0, 0)),
        ],
        out_specs=pl.BlockSpec((1, tm, d), lambda bb, i: (bb, i, 0)),
        out_shape=jax.ShapeDtypeStruct(x.shape, f32),
        scratch_shapes=[pltpu.VMEM((len(POOL_WINDOWS), tm + 3 * POOL_HALO, POOL_WIDTH), f32)],
        compiler_params=_cparams(("parallel", "parallel")),
        name="mixout",
    )(x, mod, att, u, u, u, w_pool, pool_scale.reshape(1, POOL_WIDTH), w_out)


def _rope_tables(n):
    rows = n // GRID_W
    row = jnp.repeat(jnp.arange(rows, dtype=f32), GRID_W)
    col = jnp.tile(jnp.arange(GRID_W, dtype=f32), rows)
    nf = QK_DIM // 4
    freqs = ROPE_BASE ** (-jnp.arange(nf, dtype=f32) / nf)
    ar = row[:, None] * freqs
    ac = col[:, None] * freqs
    ang = jnp.concatenate([ar, ar, ac, ac], axis=-1)
    sign = jnp.tile(jnp.concatenate([-jnp.ones((nf,), f32), jnp.ones((nf,), f32)]), 2)
    cos = jnp.tile(jnp.cos(ang), (1, 2))
    sin_signed = jnp.tile(jnp.sin(ang) * sign, (1, 2))
    return cos, sin_signed


def kernel(x, c, ctx, c_ctx, w_mod, b_mod, g_ffn1, ffn1_w_in, ffn1_w_out, g_mix, w_in,
           lambda_q1, lambda_k1, lambda_q2, lambda_k2, g_sub, w_pool, pool_scale, w_out,
           g_ffn2, ffn2_w_in, ffn2_w_out, g_final):
    b, n, d = x.shape
    l = 0
    cond = jnp.zeros((MOD_ROWS, d), f32).at[:b].set(c).at[b].set(c_ctx)
    mod = _mod_call(cond, w_mod[l], b_mod[l]).reshape(MOD_ROWS, N_MOD, d)
    lat_row = lambda bb: bb
    ctx_row = lambda bb: b

    w1_in, w1_out = ffn1_w_in[l].astype(bf16), ffn1_w_out[l].astype(bf16)
    w2_in, w2_out = ffn2_w_in[l].astype(bf16), ffn2_w_out[l].astype(bf16)
    w_in_b = w_in[l].astype(bf16)
    w_o_b = w_out[l].astype(bf16)
    w_pool_b = w_pool[l].astype(bf16)

    x1 = _ffn_call(x, mod, lat_row, 0, g_ffn1[l], w1_in, w1_out)
    cx1 = _ffn_call(ctx, mod, ctx_row, 0, g_ffn1[l], w1_in, w1_out)

    qt, k_lat, vt_lat, u = _inproj_call(x1, mod, lat_row, g_mix[l], w_in_b, _rope_tables(n))
    k_ctx, vt_ctx = _inproj_call(cx1, mod, ctx_row, g_mix[l],
                                 w_in_b[:, ATT_WIDTH:3 * ATT_WIDTH])

    lam_rows = jnp.stack([lambda_q1[l], lambda_k1[l], lambda_q2[l], lambda_k2[l]]).astype(f32)
    att = _attn_call(lam_rows, g_sub[l], qt, k_lat, k_ctx, vt_lat, vt_ctx)

    x2 = _mixout_call(x1, mod, att, u, w_pool_b, pool_scale[l], w_o_b)
    return _ffn_call(x2, mod, lat_row, 6, g_ffn2[l], w2_in, w2_out, g_final)
```

```python
import functools
import math

import jax
import jax.numpy as jnp
from jax import lax
from jax.experimental import pallas as pl
from jax.experimental.pallas import tpu as pltpu

D_MODEL = 1024
GRID_W = 64
ATT_WIDTH = 512
POOL_WIDTH = 512
ATT_HEADS = 4
HEAD_DIM = 128
QK_DIM = 64
POOL_WINDOWS = (2, 4, 8, 16)
POOL_GROUP_DIM = 128
assert POOL_WINDOWS == tuple(2 << g for g in range(len(POOL_WINDOWS)))
D_FF = 2816
N_MOD = 9
ROPE_BASE = 10000.0
EPS = 1e-6
LAMBDA_INIT = 0.8 - 0.6 * math.exp(-0.3 * 0)

MXU_COLS = 256
SLABS_PER_TILE = MXU_COLS // HEAD_DIM
MOD_ROWS = 16
POOL_HALO = 8
NEG_BIG = -1e30
Q_SCALE = (QK_DIM ** -0.5) * math.log2(math.e)

VMEM_LIMIT = 56 * 1024 * 1024

TM_FFN = 512
TM_PROJ = 512
TM_MIX = 512
TQ_ATT = 1024
TK_ATT = 1024
TIE_BLOCK = 2
QT_ATT = 2
VT_ROWS = HEAD_DIM + 16
FF_CHUNKS = (512, 512, 512, 512, 512, 256)

bf16 = jnp.bfloat16
f32 = jnp.float32


def _cparams(sem, flags=None):
    return pltpu.CompilerParams(dimension_semantics=sem, vmem_limit_bytes=VMEM_LIMIT,
                                flags=flags)


def _zero_after(v):
    bits = pltpu.bitcast(v, jnp.uint32)
    zero = lax.shift_right_logical(lax.shift_right_logical(bits, jnp.uint32(16)), jnp.uint32(16))
    return pltpu.bitcast(zero, f32)


def _norm_mod(x, g, shift, scale):
    ms = jnp.mean(x * x, axis=-1, keepdims=True)
    y = x * lax.rsqrt(ms + EPS) * g
    return y * (1.0 + scale) + shift


def _mod_kernel(cond_ref, w_ref, b_ref, o_ref):
    cnd = cond_ref[...]
    a = (cnd * jax.nn.sigmoid(cnd)).astype(bf16)
    o_ref[...] = jnp.dot(a, w_ref[...].astype(bf16), preferred_element_type=f32) + b_ref[...]


def _mod_call(cond, w_mod, b_mod):
    n_out = w_mod.shape[1]
    tn = D_MODEL
    return pl.pallas_call(
        _mod_kernel,
        grid=(n_out // tn,),
        in_specs=[
            pl.BlockSpec((MOD_ROWS, D_MODEL), lambda j: (0, 0)),
            pl.BlockSpec((D_MODEL, tn), lambda j: (0, j)),
            pl.BlockSpec((1, tn), lambda j: (0, j)),
        ],
        out_specs=pl.BlockSpec((MOD_ROWS, tn), lambda j: (0, j)),
        out_shape=jax.ShapeDtypeStruct((MOD_ROWS, n_out), f32),
        compiler_params=_cparams(("arbitrary",)),
        name="mod",
    )(cond, w_mod, b_mod.reshape(1, n_out))


def _bf16(w):
    return w if w.dtype == bf16 else w.astype(bf16)


def _swiglu_rows(x, g, shift, scale, gate, win_ref, wout_ref):
    h = _norm_mod(x, g, shift, scale).astype(bf16)
    acc = None
    off = 0
    for tf in FF_CHUNKS:
        gt = jnp.dot(h, _bf16(win_ref[:, off:off + tf]), preferred_element_type=f32)
        up = jnp.dot(h, _bf16(win_ref[:, D_FF + off:D_FF + off + tf]),
                     preferred_element_type=f32)
        a = (gt * jax.nn.sigmoid(gt) * up).astype(bf16)
        part = jnp.dot(a, _bf16(wout_ref[off:off + tf, :]), preferred_element_type=f32)
        acc = part if acc is None else acc + part
        off += tf
    return x + 0.5 * gate * acc


def _ffn_kernel(x_ref, mod_ref, g_ref, win_ref, wout_ref, o_ref, *, k0):
    o_ref[0] = _swiglu_rows(x_ref[0], g_ref[...], mod_ref[0, k0:k0 + 1, :],
                            mod_ref[0, k0 + 1:k0 + 2, :], mod_ref[0, k0 + 2:k0 + 3, :],
                            win_ref, wout_ref)


def _ffn_call(x, mod, row_of_batch, k0, g, w_in, w_out):
    bx, t, d = x.shape
    tm = min(TM_FFN, t)
    return pl.pallas_call(
        functools.partial(_ffn_kernel, k0=k0),
        grid=(bx, t // tm),
        in_specs=[
            pl.BlockSpec((1, tm, d), lambda b, i: (b, i, 0)),
            pl.BlockSpec((1, N_MOD, d), lambda b, i: (row_of_batch(b), 0, 0)),
            pl.BlockSpec((1, d), lambda b, i: (0, 0)),
            pl.BlockSpec(w_in.shape, lambda b, i: (0, 0), pipeline_mode=pl.Buffered(1)),
            pl.BlockSpec(w_out.shape, lambda b, i: (0, 0), pipeline_mode=pl.Buffered(1)),
        ],
        out_specs=pl.BlockSpec((1, tm, d), lambda b, i: (b, i, 0)),
        out_shape=jax.ShapeDtypeStruct(x.shape, f32),
        compiler_params=_cparams(("parallel", "parallel")),
        name="ffn",
    )(x, mod, g.reshape(1, d), w_in, w_out)


def _rope(t, cos, sin_signed, first_half):
    w = t.shape[-1]
    partner = jnp.where(first_half, pltpu.roll(t, w - 16, axis=1), pltpu.roll(t, 16, axis=1))
    return t * cos + partner * sin_signed


def _inproj_kernel(x_ref, mod_ref, g_ref, w_ref, *rest, latent):
    if latent:
        cos_ref, sin_ref, qt_ref, k_ref, vt_ref, u_ref = rest
    else:
        k_ref, vt_ref = rest
    x = x_ref[0]
    shift = mod_ref[0, 3:4, :]
    scale = mod_ref[0, 4:5, :]
    h = _norm_mod(x, g_ref[...], shift, scale).astype(bf16)
    hx = jnp.dot(h, w_ref[...], preferred_element_type=f32)
    tm = x.shape[0]
    lane = lax.broadcasted_iota(jnp.int32, (tm, HEAD_DIM), 1)
    map1 = lane < QK_DIM
    if latent:
        cos = jnp.concatenate([cos_ref[...]] * ATT_HEADS, axis=1)
        sin_signed = jnp.concatenate([sin_ref[...]] * ATT_HEADS, axis=1)
        lane_w = lax.broadcasted_iota(jnp.int32, (tm, ATT_WIDTH), 1)
        first_half = (lane_w % 32) < 16
        q = _rope(hx[:, :ATT_WIDTH], cos, sin_signed, first_half) * Q_SCALE
        k = _rope(hx[:, ATT_WIDTH:2 * ATT_WIDTH], cos, sin_signed, first_half)
        v = hx[:, 2 * ATT_WIDTH:3 * ATT_WIDTH]
        u_ref[0] = hx[:, 3 * ATT_WIDTH:]
    else:
        k = hx[:, :ATT_WIDTH]
        v = hx[:, ATT_WIDTH:]
    pad_row = lax.broadcasted_iota(jnp.int32, (VT_ROWS - HEAD_DIM, tm), 0)
    ones_pad = jnp.where(pad_row == 0, 1.0, 0.0).astype(bf16)
    for hh in range(ATT_HEADS):
        hs = slice(hh * HEAD_DIM, (hh + 1) * HEAD_DIM)
        kh = k[:, hs]
        k_ref[0, hh, 0] = jnp.where(map1, kh, 0.0).astype(bf16)
        k_ref[0, hh, 1] = jnp.where(map1, 0.0, kh).astype(bf16)
        vt_ref[0, hh, 0:HEAD_DIM, :] = v[:, hs].T.astype(bf16)
        vt_ref[0, hh, HEAD_DIM:, :] = ones_pad
        if latent:
            qt_ref[0, hh] = q[:, hs].T.astype(bf16)


def _inproj_call(x, mod, row_of_batch, g, w, rope_tabs=None):
    bx, t, d = x.shape
    tm = min(TM_PROJ, t)
    latent = rope_tabs is not None
    in_specs = [
        pl.BlockSpec((1, tm, d), lambda b, i: (b, i, 0)),
        pl.BlockSpec((1, N_MOD, d), lambda b, i: (row_of_batch(b), 0, 0)),
        pl.BlockSpec((1, d), lambda b, i: (0, 0)),
        pl.BlockSpec(w.shape, lambda b, i: (0, 0), pipeline_mode=pl.Buffered(1)),
    ]
    args = [x, mod, g.reshape(1, d), w]
    k_spec = pl.BlockSpec((1, ATT_HEADS, 2, tm, HEAD_DIM), lambda b, i: (b, 0, 0, i, 0))
    k_shape = jax.ShapeDtypeStruct((bx, ATT_HEADS, 2, t, HEAD_DIM), bf16)
    vt_spec = pl.BlockSpec((1, ATT_HEADS, VT_ROWS, tm), lambda b, i: (b, 0, 0, i))
    vt_shape = jax.ShapeDtypeStruct((bx, ATT_HEADS, VT_ROWS, t), bf16)
    if latent:
        in_specs += [pl.BlockSpec((tm, HEAD_DIM), lambda b, i: (i, 0))] * 2
        args += list(rope_tabs)
        out_specs = [pl.BlockSpec((1, ATT_HEADS, HEAD_DIM, tm), lambda b, i: (b, 0, 0, i)),
                     k_spec, vt_spec,
                     pl.BlockSpec((1, tm, POOL_WIDTH), lambda b, i: (b, i, 0))]
        out_shape = [jax.ShapeDtypeStruct((bx, ATT_HEADS, HEAD_DIM, t), bf16), k_shape, vt_shape,
                     jax.ShapeDtypeStruct((bx, t, POOL_WIDTH), f32)]
    else:
        out_specs = [k_spec, vt_spec]
        out_shape = [k_shape, vt_shape]
    return pl.pallas_call(
        functools.partial(_inproj_kernel, latent=latent),
        grid=(bx, t // tm),
        in_specs=in_specs,
        out_specs=out_specs,
        out_shape=out_shape,
        compiler_params=_cparams(("parallel", "parallel")),
        name="inproj_lat" if latent else "inproj_ctx",
    )(*args)


def _attn_kernel(lam_ref, gsub_ref, qt_ref, kl_ref, kc_ref, vtl_ref, vtc_ref, o_ref,
                 s_scr, cm_scr, m_scr, acc_scr):
    tq = s_scr.shape[2] * HEAD_DIM
    n_lat = kl_ref.shape[3]
    m_scr[...] = jnp.full(m_scr.shape, NEG_BIG, f32)
    acc_scr[...] = jnp.zeros(acc_scr.shape, f32)

    n_col = tq // MXU_COLS
    n_chunks = n_lat // TK_ATT
    stages = [(ti, j) for ti in range(QT_ATT) for j in range(n_chunks + 1)]

    def scores(s, mi, t, after=None):
        ti, j = stages[s]
        keys = keys_of(j, mi)
        tk = keys.shape[0]
        q_tile = qt_ref[0, 0, :, ti * tq + t * MXU_COLS:ti * tq + (t + 1) * MXU_COLS]
        if after is not None:
            q_tile = q_tile + _zero_after(after).astype(bf16)
        st = jnp.dot(keys, q_tile, preferred_element_type=f32)
        for c in range(SLABS_PER_TILE):
            s_scr[s % 2, mi, t * SLABS_PER_TILE + c, 0:tk, :] = (
                st[:, c * HEAD_DIM:(c + 1) * HEAD_DIM])
        cm_scr[s % 2, mi, :, t * MXU_COLS:(t + 1) * MXU_COLS] = (
            jnp.max(st.reshape(tk // 8, 8, MXU_COLS), axis=0))

    def softmax_pv(s, mi, t):
        ti, j = stages[s]
        slot = s % 2
        vt = values_of(j)
        tk = vt.shape[1]
        ts = slice(t * MXU_COLS, (t + 1) * MXU_COLS)
        m_old = m_scr[ti, mi, :, ts]
        m_new = jnp.maximum(m_old, jnp.max(cm_scr[slot, mi, :, ts], axis=0, keepdims=True))
        m_scr[ti, mi, :, ts] = m_new
        alpha = jnp.exp2(m_old - m_new)
        mb = [jnp.broadcast_to(m_new[:, c * HEAD_DIM:(c + 1) * HEAD_DIM], (MXU_COLS, HEAD_DIM))
              for c in range(SLABS_PER_TILE)]
        pv = None
        done = None
        n_blocks = tk // MXU_COLS
        for bi in range(n_blocks):
            rs = slice(bi * MXU_COLS, (bi + 1) * MXU_COLS)
            e = [jnp.exp2(s_scr[slot, mi, t * SLABS_PER_TILE + c, rs, :] - mb[c])
                 for c in range(SLABS_PER_TILE)]
            p = jnp.concatenate([ec.astype(bf16) for ec in e], axis=1)
            part = jnp.dot(vt[:, rs], p, preferred_element_type=f32)
            pv = part if pv is None else pv + part
            if bi == min(TIE_BLOCK, n_blocks - 1):
                done = jnp.concatenate([jnp.max(ec, axis=0, keepdims=True) for ec in e], axis=1)
        acc_scr[ti, mi, :, ts] = alpha * acc_scr[ti, mi, :, ts] + pv
        return done

    def keys_of(j, mi):
        if j < n_chunks:
            return kl_ref[0, 0, mi, j * TK_ATT:(j + 1) * TK_ATT, :]
        return kc_ref[0, 0, mi]

    def values_of(j):
        if j < n_chunks:
            return vtl_ref[0, 0, :, j * TK_ATT:(j + 1) * TK_ATT]
        return vtc_ref[0, 0]

    def finalize(ti):
        lam_v = lam_ref[...]
        lam = (jnp.exp(jnp.sum(lam_v[0:1] * lam_v[1:2], axis=-1, keepdims=True))
               - jnp.exp(jnp.sum(lam_v[2:3] * lam_v[3:4], axis=-1, keepdims=True))
               + LAMBDA_INIT)
        o1 = acc_scr[ti, 0, 0:HEAD_DIM, :] / acc_scr[ti, 0, HEAD_DIM:HEAD_DIM + 1, :]
        o2 = acc_scr[ti, 1, 0:HEAD_DIM, :] / acc_scr[ti, 1, HEAD_DIM:HEAD_DIM + 1, :]
        o = o1 - lam * o2
        ms = jnp.mean(o * o, axis=0, keepdims=True)
        o = o * lax.rsqrt(ms + EPS) * (gsub_ref[...] * (1.0 - LAMBDA_INIT))
        o_ref[0, ti * tq:(ti + 1) * tq, :] = o.T.astype(bf16)

    for mi in range(2):
        for t in range(n_col):
            scores(0, mi, t)
    prev_done = None
    for s in range(len(stages)):
        for mi in range(2):
            for t in range(n_col):
                if s + 1 < len(stages):
                    scores(s + 1, mi, t, after=prev_done)
                prev_done = softmax_pv(s, mi, t)
        if stages[s][1] == n_chunks:
            finalize(stages[s][0])


def _attn_call(lam_rows, g_sub, qt, k_lat, k_ctx, vt_lat, vt_ctx):
    b, _, _, n = qt.shape
    n_ctx = k_ctx.shape[3]
    tq = TQ_ATT
    tstep = QT_ATT * tq
    return pl.pallas_call(
        _attn_kernel,
        grid=(b, ATT_HEADS, n // tstep),
        in_specs=[
            pl.BlockSpec((4, QK_DIM), lambda bb, h, i: (0, 0)),
            pl.BlockSpec((HEAD_DIM, 1), lambda bb, h, i: (0, 0)),
            pl.BlockSpec((1, 1, HEAD_DIM, tstep), lambda bb, h, i: (bb, h, 0, i)),
            pl.BlockSpec((1, 1, 2, n, HEAD_DIM), lambda bb, h, i: (bb, h, 0, 0, 0)),
            pl.BlockSpec((1, 1, 2, n_ctx, HEAD_DIM), lambda bb, h, i: (bb, h, 0, 0, 0)),
            pl.BlockSpec((1, 1, VT_ROWS, n), lambda bb, h, i: (bb, h, 0, 0)),
            pl.BlockSpec((1, 1, VT_ROWS, n_ctx), lambda bb, h, i: (bb, h, 0, 0)),
        ],
        out_specs=pl.BlockSpec((1, tstep, HEAD_DIM), lambda bb, h, i: (bb, i, h)),
        out_shape=jax.ShapeDtypeStruct((b, n, ATT_WIDTH), bf16),
        scratch_shapes=[
            pltpu.VMEM((2, 2, tq // HEAD_DIM, TK_ATT, HEAD_DIM), f32),
            pltpu.VMEM((2, 2, 8, tq), f32),
            pltpu.VMEM((QT_ATT, 2, 1, tq), f32),
            pltpu.VMEM((QT_ATT, 2, VT_ROWS, tq), f32),
        ],
        compiler_params=_cparams(("parallel", "parallel", "arbitrary")),
        name="attn",
    )(lam_rows, g_sub.reshape(HEAD_DIM, 1), qt, k_lat, k_ctx, vt_lat, vt_ctx)


def _mixffn_kernel(x_ref, mod_ref, att_ref, u_ref, up_ref, un_ref, wp_ref, ps_ref, wo_ref,
                   g2_ref, win_ref, wout_ref, gf_ref, o_ref, run, *, n_tok):
    i = pl.program_id(1)
    nt = pl.num_programs(1)
    tm = x_ref.shape[1]
    hl = POOL_HALO
    y_att = jnp.dot(att_ref[0], wo_ref[0:ATT_WIDTH, :], preferred_element_type=f32)
    base = 2 * hl
    top = base + tm + hl
    zeros = jnp.zeros((hl, POOL_WIDTH), f32)
    run[0, 0:hl, :] = zeros
    run[0, hl:base, :] = jnp.where(i > 0, up_ref[0], 0.0)
    run[0, base:base + tm, :] = u_ref[0]
    run[0, base + tm:top, :] = jnp.where(i < nt - 1, un_ref[0], 0.0)
    for k in range(1, len(POOL_WINDOWS)):
        reach = 1 << (k - 1)
        lanes = slice((k - 1) * POOL_GROUP_DIM, POOL_WIDTH)
        run[k, 0:hl, :] = zeros
        run[k, hl:top, lanes] = run[k - 1, hl - reach:top - reach, lanes] + run[k - 1, hl:top, lanes]

    t_head = i * tm + lax.broadcasted_iota(jnp.int32, (hl, 1), 0)
    t_tail = t_head + (tm - hl)
    pooled = []
    for gi, w in enumerate(POOL_WINDOWS):
        lo = w // 2
        hi = w - lo - 1
        sl = slice(gi * POOL_GROUP_DIM, (gi + 1) * POOL_GROUP_DIM)
        if gi + 1 < len(POOL_WINDOWS):
            total = run[gi + 1, base + hi:base + hi + tm, sl]
        else:
            half = w // 2
            total = (run[gi, base + hi - half:base + hi - half + tm, sl]
                     + run[gi, base + hi:base + hi + tm, sl])

        def cnt(t):
            return (jnp.minimum(t + hi, n_tok - 1) - jnp.maximum(t - lo, 0) + 1).astype(f32)

        mean = jnp.concatenate([total[0:hl] / cnt(t_head),
                                total[hl:tm - hl] * (1.0 / w),
                                total[tm - hl:tm] / cnt(t_tail)], axis=0)
        diff = (mean - run[0, base:base + tm, sl]).astype(bf16)
        pooled.append(jnp.dot(diff, wp_ref[gi], preferred_element_type=f32))
    pool = (jnp.concatenate(pooled, axis=-1) * ps_ref[...]).astype(bf16)
    y = y_att + jnp.dot(pool, wo_ref[ATT_WIDTH:, :], preferred_element_type=f32)
    x2 = x_ref[0] + mod_ref[0, 5:6, :] * y
    y = _swiglu_rows(x2, g2_ref[...], mod_ref[0, 6:7, :], mod_ref[0, 7:8, :], mod_ref[0, 8:9, :],
                     win_ref, wout_ref)
    ms = jnp.mean(y * y, axis=-1, keepdims=True)
    o_ref[0] = y * lax.rsqrt(ms + EPS) * gf_ref[...]


def _mixffn_call(x, mod, att, u, w_pool, pool_scale, w_out, g2, w2_in, w2_out, g_final):
    b, n, d = x.shape
    tm = TM_MIX
    hb = tm // POOL_HALO
    last_halo_block = n // POOL_HALO - 1
    const2 = lambda bb, i: (0, 0)
    return pl.pallas_call(
        functools.partial(_mixffn_kernel, n_tok=n),
        grid=(b, n // tm),
        in_specs=[
            pl.BlockSpec((1, tm, d), lambda bb, i: (bb, i, 0)),
            pl.BlockSpec((1, N_MOD, d), lambda bb, i: (bb, 0, 0)),
            pl.BlockSpec((1, tm, ATT_WIDTH), lambda bb, i: (bb, i, 0)),
            pl.BlockSpec((1, tm, POOL_WIDTH), lambda bb, i: (bb, i, 0)),
            pl.BlockSpec((1, POOL_HALO, POOL_WIDTH),
                         lambda bb, i: (bb, jnp.maximum(i * hb - 1, 0), 0)),
            pl.BlockSpec((1, POOL_HALO, POOL_WIDTH),
                         lambda bb, i: (bb, jnp.minimum((i + 1) * hb, last_halo_block), 0)),
            pl.BlockSpec(w_pool.shape, lambda bb, i: (0, 0, 0), pipeline_mode=pl.Buffered(1)),
            pl.BlockSpec((1, POOL_WIDTH), const2),
            pl.BlockSpec(w_out.shape, const2, pipeline_mode=pl.Buffered(1)),
            pl.BlockSpec((1, d), const2),
            pl.BlockSpec(w2_in.shape, const2, pipeline_mode=pl.Buffered(1)),
            pl.BlockSpec(w2_out.shape, const2, pipeline_mode=pl.Buffered(1)),
            pl.BlockSpec((1, d), const2),
        ],
        out_specs=pl.BlockSpec((1, tm, d), lambda bb, i: (bb, i, 0)),
        out_shape=jax.ShapeDtypeStruct(x.shape, f32),
        scratch_shapes=[pltpu.VMEM((len(POOL_WINDOWS), tm + 3 * POOL_HALO, POOL_WIDTH), f32)],
        compiler_params=_cparams(("parallel", "parallel")),
        name="mixffn",
    )(x, mod, att, u, u, u, w_pool, pool_scale.reshape(1, POOL_WIDTH), w_out,
      g2.reshape(1, d), w2_in, w2_out, g_final.reshape(1, d))


def _rope_tables(n):
    rows = n // GRID_W
    row = jnp.repeat(jnp.arange(rows, dtype=f32), GRID_W)
    col = jnp.tile(jnp.arange(GRID_W, dtype=f32), rows)
    nf = QK_DIM // 4
    freqs = ROPE_BASE ** (-jnp.arange(nf, dtype=f32) / nf)
    ar = row[:, None] * freqs
    ac = col[:, None] * freqs
    ang = jnp.concatenate([ar, ar, ac, ac], axis=-1)
    sign = jnp.tile(jnp.concatenate([-jnp.ones((nf,), f32), jnp.ones((nf,), f32)]), 2)
    cos = jnp.tile(jnp.cos(ang), (1, 2))
    sin_signed = jnp.tile(jnp.sin(ang) * sign, (1, 2))
    return cos, sin_signed


def kernel(x, c, ctx, c_ctx, w_mod, b_mod, g_ffn1, ffn1_w_in, ffn1_w_out, g_mix, w_in,
           lambda_q1, lambda_k1, lambda_q2, lambda_k2, g_sub, w_pool, pool_scale, w_out,
           g_ffn2, ffn2_w_in, ffn2_w_out, g_final):
    b, n, d = x.shape
    l = 0
    cond = jnp.zeros((MOD_ROWS, d), f32).at[:b].set(c).at[b].set(c_ctx)
    mod = _mod_call(cond, w_mod[l], b_mod[l]).reshape(MOD_ROWS, N_MOD, d)
    lat_row = lambda bb: bb
    ctx_row = lambda bb: b

    w2_in, w2_out = ffn2_w_in[l].astype(bf16), ffn2_w_out[l].astype(bf16)
    w_in_b = w_in[l].astype(bf16)
    w_o_b = w_out[l].astype(bf16)
    w_pool_b = w_pool[l].astype(bf16)

    x1 = _ffn_call(x, mod, lat_row, 0, g_ffn1[l], ffn1_w_in[l], ffn1_w_out[l])
    cx1 = _ffn_call(ctx, mod, ctx_row, 0, g_ffn1[l], ffn1_w_in[l], ffn1_w_out[l])

    qt, k_lat, vt_lat, u = _inproj_call(x1, mod, lat_row, g_mix[l], w_in_b, _rope_tables(n))
    k_ctx, vt_ctx = _inproj_call(cx1, mod, ctx_row, g_mix[l],
                                 w_in_b[:, ATT_WIDTH:3 * ATT_WIDTH])

    lam_rows = jnp.stack([lambda_q1[l], lambda_k1[l], lambda_q2[l], lambda_k2[l]]).astype(f32)
    att = _attn_call(lam_rows, g_sub[l], qt, k_lat, k_ctx, vt_lat, vt_ctx)

    return _mixffn_call(x1, mod, att, u, w_pool_b, pool_scale[l], w_o_b,
                        g_ffn2[l], w2_in, w2_out, g_final)
```

```python
import functools
import math

import jax
import jax.numpy as jnp
from jax import lax
from jax.experimental import pallas as pl
from jax.experimental.pallas import tpu as pltpu

D_MODEL = 1024
GRID_W = 64
ATT_WIDTH = 512
POOL_WIDTH = 512
ATT_HEADS = 4
HEAD_DIM = 128
QK_DIM = 64
POOL_WINDOWS = (2, 4, 8, 16)
POOL_GROUP_DIM = 128
assert POOL_WINDOWS == tuple(2 << g for g in range(len(POOL_WINDOWS)))
D_FF = 2816
N_MOD = 9
ROPE_BASE = 10000.0
EPS = 1e-6
LAMBDA_INIT = 0.8 - 0.6 * math.exp(-0.3 * 0)

MXU_COLS = 256
SLABS_PER_TILE = MXU_COLS // HEAD_DIM
MOD_ROWS = 16
POOL_HALO = 8
NEG_BIG = -1e30
Q_SCALE = (QK_DIM ** -0.5) * math.log2(math.e)

VMEM_LIMIT = 56 * 1024 * 1024

TM_FFN = 512
TM_PROJ = 512
TM_MIX = 512
TQ_ATT = 1024
TK_ATT = 1024
TIE_BLOCK = 2
QT_ATT = 2
VT_ROWS = HEAD_DIM + 16
FF_CHUNKS = (512, 512, 512, 512, 512, 256)

bf16 = jnp.bfloat16
f32 = jnp.float32


def _cparams(sem, flags=None):
    return pltpu.CompilerParams(dimension_semantics=sem, vmem_limit_bytes=VMEM_LIMIT,
                                flags=flags)


def _zero_after(v):
    bits = pltpu.bitcast(v, jnp.uint32)
    zero = lax.shift_right_logical(lax.shift_right_logical(bits, jnp.uint32(16)), jnp.uint32(16))
    return pltpu.bitcast(zero, f32)


def _norm_mod(x, g, shift, scale):
    ms = jnp.mean(x * x, axis=-1, keepdims=True)
    y = x * lax.rsqrt(ms + EPS) * g
    return y * (1.0 + scale) + shift


def _mod_kernel(cond_ref, w_ref, b_ref, o_ref):
    cnd = cond_ref[...]
    a = (cnd * jax.nn.sigmoid(cnd)).astype(bf16)
    o_ref[...] = jnp.dot(a, w_ref[...].astype(bf16), preferred_element_type=f32) + b_ref[...]


def _mod_call(cond, w_mod, b_mod):
    n_out = w_mod.shape[1]
    tn = D_MODEL
    return pl.pallas_call(
        _mod_kernel,
        grid=(n_out // tn,),
        in_specs=[
            pl.BlockSpec((MOD_ROWS, D_MODEL), lambda j: (0, 0)),
            pl.BlockSpec((D_MODEL, tn), lambda j: (0, j)),
            pl.BlockSpec((1, tn), lambda j: (0, j)),
        ],
        out_specs=pl.BlockSpec((MOD_ROWS, tn), lambda j: (0, j)),
        out_shape=jax.ShapeDtypeStruct((MOD_ROWS, n_out), f32),
        compiler_params=_cparams(("arbitrary",)),
        name="mod",
    )(cond, w_mod, b_mod.reshape(1, n_out))


def _bf16(w):
    return w if w.dtype == bf16 else w.astype(bf16)


def _swiglu_rows(x, g, shift, scale, gate, win_ref, wout_ref):
    h = _norm_mod(x, g, shift, scale).astype(bf16)
    acc = None
    off = 0
    for tf in FF_CHUNKS:
        gt = jnp.dot(h, _bf16(win_ref[:, off:off + tf]), preferred_element_type=f32)
        up = jnp.dot(h, _bf16(win_ref[:, D_FF + off:D_FF + off + tf]),
                     preferred_element_type=f32)
        a = (gt * jax.nn.sigmoid(gt) * up).astype(bf16)
        part = jnp.dot(a, _bf16(wout_ref[off:off + tf, :]), preferred_element_type=f32)
        acc = part if acc is None else acc + part
        off += tf
    return x + 0.5 * gate * acc


def _ffn_kernel(x_ref, mod_ref, g_ref, win_ref, wout_ref, o_ref, *, k0):
    o_ref[0] = _swiglu_rows(x_ref[0], g_ref[...], mod_ref[0, k0:k0 + 1, :],
                            mod_ref[0, k0 + 1:k0 + 2, :], mod_ref[0, k0 + 2:k0 + 3, :],
                            win_ref, wout_ref)


def _ffn_call(x, mod, row_of_batch, k0, g, w_in, w_out):
    bx, t, d = x.shape
    tm = min(TM_FFN, t)
    return pl.pallas_call(
        functools.partial(_ffn_kernel, k0=k0),
        grid=(bx, t // tm),
        in_specs=[
            pl.BlockSpec((1, tm, d), lambda b, i: (b, i, 0)),
            pl.BlockSpec((1, N_MOD, d), lambda b, i: (row_of_batch(b), 0, 0)),
            pl.BlockSpec((1, d), lambda b, i: (0, 0)),
            pl.BlockSpec(w_in.shape, lambda b, i: (0, 0), pipeline_mode=pl.Buffered(1)),
            pl.BlockSpec(w_out.shape, lambda b, i: (0, 0), pipeline_mode=pl.Buffered(1)),
        ],
        out_specs=pl.BlockSpec((1, tm, d), lambda b, i: (b, i, 0)),
        out_shape=jax.ShapeDtypeStruct(x.shape, f32),
        compiler_params=_cparams(("parallel", "parallel")),
        name="ffn",
    )(x, mod, g.reshape(1, d), w_in, w_out)


def _rope(t, cos, sin_signed, first_half):
    w = t.shape[-1]
    partner = jnp.where(first_half, pltpu.roll(t, w - 16, axis=1), pltpu.roll(t, 16, axis=1))
    return t * cos + partner * sin_signed


def _inproj_kernel(x_ref, mod_ref, g_ref, w_ref, *rest, latent):
    if latent:
        cos_ref, sin_ref, qt_ref, k_ref, vt_ref, u_ref = rest
    else:
        k_ref, vt_ref = rest
    x = x_ref[0]
    shift = mod_ref[0, 3:4, :]
    scale = mod_ref[0, 4:5, :]
    h = _norm_mod(x, g_ref[...], shift, scale).astype(bf16)
    hx = jnp.dot(h, w_ref[...], preferred_element_type=f32)
    tm = x.shape[0]
    lane = lax.broadcasted_iota(jnp.int32, (tm, HEAD_DIM), 1)
    map1 = lane < QK_DIM
    if latent:
        cos = jnp.concatenate([cos_ref[...]] * ATT_HEADS, axis=1)
        sin_signed = jnp.concatenate([sin_ref[...]] * ATT_HEADS, axis=1)
        lane_w = lax.broadcasted_iota(jnp.int32, (tm, ATT_WIDTH), 1)
        first_half = (lane_w % 32) < 16
        q = _rope(hx[:, :ATT_WIDTH], cos, sin_signed, first_half) * Q_SCALE
        k = _rope(hx[:, ATT_WIDTH:2 * ATT_WIDTH], cos, sin_signed, first_half)
        v = hx[:, 2 * ATT_WIDTH:3 * ATT_WIDTH]
        u_ref[0] = hx[:, 3 * ATT_WIDTH:]
    else:
        k = hx[:, :ATT_WIDTH]
        v = hx[:, ATT_WIDTH:]
    pad_row = lax.broadcasted_iota(jnp.int32, (VT_ROWS - HEAD_DIM, tm), 0)
    ones_pad = jnp.where(pad_row == 0, 1.0, 0.0).astype(bf16)
    for hh in range(ATT_HEADS):
        hs = slice(hh * HEAD_DIM, (hh + 1) * HEAD_DIM)
        kh = k[:, hs]
        k_ref[0, hh, 0] = jnp.where(map1, kh, 0.0).astype(bf16)
        k_ref[0, hh, 1] = jnp.where(map1, 0.0, kh).astype(bf16)
        vt_ref[0, hh, 0:HEAD_DIM, :] = v[:, hs].T.astype(bf16)
        vt_ref[0, hh, HEAD_DIM:, :] = ones_pad
        if latent:
            qt_ref[0, hh] = q[:, hs].T.astype(bf16)


def _inproj_call(x, mod, row_of_batch, g, w, rope_tabs=None):
    bx, t, d = x.shape
    tm = min(TM_PROJ, t)
    latent = rope_tabs is not None
    in_specs = [
        pl.BlockSpec((1, tm, d), lambda b, i: (b, i, 0)),
        pl.BlockSpec((1, N_MOD, d), lambda b, i: (row_of_batch(b), 0, 0)),
        pl.BlockSpec((1, d), lambda b, i: (0, 0)),
        pl.BlockSpec(w.shape, lambda b, i: (0, 0), pipeline_mode=pl.Buffered(1)),
    ]
    args = [x, mod, g.reshape(1, d), w]
    k_spec = pl.BlockSpec((1, ATT_HEADS, 2, tm, HEAD_DIM), lambda b, i: (b, 0, 0, i, 0))
    k_shape = jax.ShapeDtypeStruct((bx, ATT_HEADS, 2, t, HEAD_DIM), bf16)
    vt_spec = pl.BlockSpec((1, ATT_HEADS, VT_ROWS, tm), lambda b, i: (b, 0, 0, i))
    vt_shape = jax.ShapeDtypeStruct((bx, ATT_HEADS, VT_ROWS, t), bf16)
    if latent:
        in_specs += [pl.BlockSpec((tm, HEAD_DIM), lambda b, i: (i, 0))] * 2
        args += list(rope_tabs)
        out_specs = [pl.BlockSpec((1, ATT_HEADS, HEAD_DIM, tm), lambda b, i: (b, 0, 0, i)),
                     k_spec, vt_spec,
                     pl.BlockSpec((1, tm, POOL_WIDTH), lambda b, i: (b, i, 0))]
        out_shape = [jax.ShapeDtypeStruct((bx, ATT_HEADS, HEAD_DIM, t), bf16), k_shape, vt_shape,
                     jax.ShapeDtypeStruct((bx, t, POOL_WIDTH), f32)]
    else:
        out_specs = [k_spec, vt_spec]
        out_shape = [k_shape, vt_shape]
    return pl.pallas_call(
        functools.partial(_inproj_kernel, latent=latent),
        grid=(bx, t // tm),
        in_specs=in_specs,
        out_specs=out_specs,
        out_shape=out_shape,
        compiler_params=_cparams(("parallel", "parallel")),
        name="inproj_lat" if latent else "inproj_ctx",
    )(*args)


def _attn_kernel(lam_ref, gsub_ref, qt_ref, kl_ref, kc_ref, vtl_ref, vtc_ref, o_ref,
                 s_scr, cm_scr, m_scr, acc_scr):
    tq = s_scr.shape[2] * HEAD_DIM
    n_lat = kl_ref.shape[3]
    m_scr[...] = jnp.full(m_scr.shape, NEG_BIG, f32)
    acc_scr[...] = jnp.zeros(acc_scr.shape, f32)

    n_col = tq // MXU_COLS
    n_chunks = n_lat // TK_ATT
    stages = [(ti, j) for ti in range(QT_ATT) for j in range(n_chunks + 1)]

    def scores(s, mi, t, after=None):
        ti, j = stages[s]
        keys = keys_of(j, mi)
        tk = keys.shape[0]
        q_tile = qt_ref[0, 0, :, ti * tq + t * MXU_COLS:ti * tq + (t + 1) * MXU_COLS]
        if after is not None:
            q_tile = q_tile + _zero_after(after).astype(bf16)
        st = jnp.dot(keys, q_tile, preferred_element_type=f32)
        for c in range(SLABS_PER_TILE):
            s_scr[s % 2, mi, t * SLABS_PER_TILE + c, 0:tk, :] = (
                st[:, c * HEAD_DIM:(c + 1) * HEAD_DIM])
        cm_scr[s % 2, mi, :, t * MXU_COLS:(t + 1) * MXU_COLS] = (
            jnp.max(st.reshape(tk // 8, 8, MXU_COLS), axis=0))

    def softmax_pv(s, mi, t):
        ti, j = stages[s]
        slot = s % 2
        vt = values_of(j)
        tk = vt.shape[1]
        ts = slice(t * MXU_COLS, (t + 1) * MXU_COLS)
        m_old = m_scr[ti, mi, :, ts]
        m_new = jnp.maximum(m_old, jnp.max(cm_scr[slot, mi, :, ts], axis=0, keepdims=True))
        m_scr[ti, mi, :, ts] = m_new
        alpha = jnp.exp2(m_old - m_new)
        mb = [jnp.broadcast_to(m_new[:, c * HEAD_DIM:(c + 1) * HEAD_DIM], (MXU_COLS, HEAD_DIM))
              for c in range(SLABS_PER_TILE)]
        pv = None
        done = None
        n_blocks = tk // MXU_COLS
        for bi in range(n_blocks):
            rs = slice(bi * MXU_COLS, (bi + 1) * MXU_COLS)
            e = [jnp.exp2(s_scr[slot, mi, t * SLABS_PER_TILE + c, rs, :] - mb[c])
                 for c in range(SLABS_PER_TILE)]
            p = jnp.concatenate([ec.astype(bf16) for ec in e], axis=1)
            part = jnp.dot(vt[:, rs], p, preferred_element_type=f32)
            pv = part if pv is None else pv + part
            if bi == min(TIE_BLOCK, n_blocks - 1):
                done = jnp.concatenate([jnp.max(ec, axis=0, keepdims=True) for ec in e], axis=1)
        for c in range(SLABS_PER_TILE):
            cs = slice(c * HEAD_DIM, (c + 1) * HEAD_DIM)
            sb = t * SLABS_PER_TILE + c
            acc_scr[ti, mi, sb] = alpha[:, cs] * acc_scr[ti, mi, sb] + pv[:, cs]
        return done

    def keys_of(j, mi):
        if j < n_chunks:
            return kl_ref[0, 0, mi, j * TK_ATT:(j + 1) * TK_ATT, :]
        return kc_ref[0, 0, mi]

    def values_of(j):
        if j < n_chunks:
            return vtl_ref[0, 0, :, j * TK_ATT:(j + 1) * TK_ATT]
        return vtc_ref[0, 0]

    def finalize(ti):
        lam_v = lam_ref[...]
        lam = (jnp.exp(jnp.sum(lam_v[0:1] * lam_v[1:2], axis=-1, keepdims=True))
               - jnp.exp(jnp.sum(lam_v[2:3] * lam_v[3:4], axis=-1, keepdims=True))
               + LAMBDA_INIT)
        def normalised(mi):
            return jnp.concatenate(
                [acc_scr[ti, mi, sb, 0:HEAD_DIM, :] / acc_scr[ti, mi, sb, HEAD_DIM:HEAD_DIM + 1, :]
                 for sb in range(tq // HEAD_DIM)], axis=1)

        o = normalised(0) - lam * normalised(1)
        ms = jnp.mean(o * o, axis=0, keepdims=True)
        o = o * lax.rsqrt(ms + EPS) * (gsub_ref[...] * (1.0 - LAMBDA_INIT))
        o_ref[0, ti * tq:(ti + 1) * tq, :] = o.T.astype(bf16)

    for mi in range(2):
        for t in range(n_col):
            scores(0, mi, t)
    prev_done = None
    for s in range(len(stages)):
        for mi in range(2):
            for t in range(n_col):
                if s + 1 < len(stages):
                    scores(s + 1, mi, t, after=prev_done)
                prev_done = softmax_pv(s, mi, t)
        if stages[s][1] == n_chunks:
            finalize(stages[s][0])


def _attn_call(lam_rows, g_sub, qt, k_lat, k_ctx, vt_lat, vt_ctx):
    b, _, _, n = qt.shape
    n_ctx = k_ctx.shape[3]
    tq = TQ_ATT
    tstep = QT_ATT * tq
    return pl.pallas_call(
        _attn_kernel,
        grid=(b, ATT_HEADS, n // tstep),
        in_specs=[
            pl.BlockSpec((4, QK_DIM), lambda bb, h, i: (0, 0)),
            pl.BlockSpec((HEAD_DIM, 1), lambda bb, h, i: (0, 0)),
            pl.BlockSpec((1, 1, HEAD_DIM, tstep), lambda bb, h, i: (bb, h, 0, i)),
            pl.BlockSpec((1, 1, 2, n, HEAD_DIM), lambda bb, h, i: (bb, h, 0, 0, 0)),
            pl.BlockSpec((1, 1, 2, n_ctx, HEAD_DIM), lambda bb, h, i: (bb, h, 0, 0, 0)),
            pl.BlockSpec((1, 1, VT_ROWS, n), lambda bb, h, i: (bb, h, 0, 0)),
            pl.BlockSpec((1, 1, VT_ROWS, n_ctx), lambda bb, h, i: (bb, h, 0, 0)),
        ],
        out_specs=pl.BlockSpec((1, tstep, HEAD_DIM), lambda bb, h, i: (bb, i, h)),
        out_shape=jax.ShapeDtypeStruct((b, n, ATT_WIDTH), bf16),
        scratch_shapes=[
            pltpu.VMEM((2, 2, tq // HEAD_DIM, TK_ATT, HEAD_DIM), f32),
            pltpu.VMEM((2, 2, 8, tq), f32),
            pltpu.VMEM((QT_ATT, 2, 1, tq), f32),
            pltpu.VMEM((QT_ATT, 2, tq // HEAD_DIM, VT_ROWS, HEAD_DIM), f32),
        ],
        compiler_params=_cparams(("parallel", "parallel", "arbitrary")),
        name="attn",
    )(lam_rows, g_sub.reshape(HEAD_DIM, 1), qt, k_lat, k_ctx, vt_lat, vt_ctx)


def _mixffn_kernel(x_ref, mod_ref, att_ref, u_ref, up_ref, un_ref, wp_ref, ps_ref, wo_ref,
                   g2_ref, win_ref, wout_ref, gf_ref, o_ref, run, *, n_tok):
    i = pl.program_id(1)
    nt = pl.num_programs(1)
    tm = x_ref.shape[1]
    hl = POOL_HALO
    y_att = jnp.dot(att_ref[0], wo_ref[0:ATT_WIDTH, :], preferred_element_type=f32)
    base = 2 * hl
    top = base + tm + hl
    zeros = jnp.zeros((hl, POOL_WIDTH), f32)
    run[0, 0:hl, :] = zeros
    run[0, hl:base, :] = jnp.where(i > 0, up_ref[0], 0.0)
    run[0, base:base + tm, :] = u_ref[0]
    run[0, base + tm:top, :] = jnp.where(i < nt - 1, un_ref[0], 0.0)
    for k in range(1, len(POOL_WINDOWS)):
        reach = 1 << (k - 1)
        lanes = slice((k - 1) * POOL_GROUP_DIM, POOL_WIDTH)
        run[k, 0:hl, :] = zeros
        run[k, hl:top, lanes] = run[k - 1, hl - reach:top - reach, lanes] + run[k - 1, hl:top, lanes]

    t_head = i * tm + lax.broadcasted_iota(jnp.int32, (hl, 1), 0)
    t_tail = t_head + (tm - hl)
    pooled = []
    for gi, w in enumerate(POOL_WINDOWS):
        lo = w // 2
        hi = w - lo - 1
        sl = slice(gi * POOL_GROUP_DIM, (gi + 1) * POOL_GROUP_DIM)
        if gi + 1 < len(POOL_WINDOWS):
            total = run[gi + 1, base + hi:base + hi + tm, sl]
        else:
            half = w // 2
            total = (run[gi, base + hi - half:base + hi - half + tm, sl]
                     + run[gi, base + hi:base + hi + tm, sl])

        def cnt(t):
            return (jnp.minimum(t + hi, n_tok - 1) - jnp.maximum(t - lo, 0) + 1).astype(f32)

        mean = jnp.concatenate([total[0:hl] / cnt(t_head),
                                total[hl:tm - hl] * (1.0 / w),
                                total[tm - hl:tm] / cnt(t_tail)], axis=0)
        diff = (mean - run[0, base:base + tm, sl]).astype(bf16)
        pooled.append(jnp.dot(diff, wp_ref[gi], preferred_element_type=f32))
    pool = (jnp.concatenate(pooled, axis=-1) * ps_ref[...]).astype(bf16)
    y = y_att + jnp.dot(pool, wo_ref[ATT_WIDTH:, :], preferred_element_type=f32)
    x2 = x_ref[0] + mod_ref[0, 5:6, :] * y
    y = _swiglu_rows(x2, g2_ref[...], mod_ref[0, 6:7, :], mod_ref[0, 7:8, :], mod_ref[0, 8:9, :],
                     win_ref, wout_ref)
    ms = jnp.mean(y * y, axis=-1, keepdims=True)
    o_ref[0] = y * lax.rsqrt(ms + EPS) * gf_ref[...]


def _mixffn_call(x, mod, att, u, w_pool, pool_scale, w_out, g2, w2_in, w2_out, g_final):
    b, n, d = x.shape
    tm = TM_MIX
    hb = tm // POOL_HALO
    last_halo_block = n // POOL_HALO - 1
    const2 = lambda bb, i: (0, 0)
    return pl.pallas_call(
        functools.partial(_mixffn_kernel, n_tok=n),
        grid=(b, n // tm),
        in_specs=[
            pl.BlockSpec((1, tm, d), lambda bb, i: (bb, i, 0)),
            pl.BlockSpec((1, N_MOD, d), lambda bb, i: (bb, 0, 0)),
            pl.BlockSpec((1, tm, ATT_WIDTH), lambda bb, i: (bb, i, 0)),
            pl.BlockSpec((1, tm, POOL_WIDTH), lambda bb, i: (bb, i, 0)),
            pl.BlockSpec((1, POOL_HALO, POOL_WIDTH),
                         lambda bb, i: (bb, jnp.maximum(i * hb - 1, 0), 0)),
            pl.BlockSpec((1, POOL_HALO, POOL_WIDTH),
                         lambda bb, i: (bb, jnp.minimum((i + 1) * hb, last_halo_block), 0)),
            pl.BlockSpec(w_pool.shape, lambda bb, i: (0, 0, 0), pipeline_mode=pl.Buffered(1)),
            pl.BlockSpec((1, POOL_WIDTH), const2),
            pl.BlockSpec(w_out.shape, const2, pipeline_mode=pl.Buffered(1)),
            pl.BlockSpec((1, d), const2),
            pl.BlockSpec(w2_in.shape, const2, pipeline_mode=pl.Buffered(1)),
            pl.BlockSpec(w2_out.shape, const2, pipeline_mode=pl.Buffered(1)),
            pl.BlockSpec((1, d), const2),
        ],
        out_specs=pl.BlockSpec((1, tm, d), lambda bb, i: (bb, i, 0)),
        out_shape=jax.ShapeDtypeStruct(x.shape, f32),
        scratch_shapes=[pltpu.VMEM((len(POOL_WINDOWS), tm + 3 * POOL_HALO, POOL_WIDTH), f32)],
        compiler_params=_cparams(("parallel", "parallel")),
        name="mixffn",
    )(x, mod, att, u, u, u, w_pool, pool_scale.reshape(1, POOL_WIDTH), w_out,
      g2.reshape(1, d), w2_in, w2_out, g_final.reshape(1, d))


def _rope_tables(n):
    rows = n // GRID_W
    row = jnp.repeat(jnp.arange(rows, dtype=f32), GRID_W)
    col = jnp.tile(jnp.arange(GRID_W, dtype=f32), rows)
    nf = QK_DIM // 4
    freqs = ROPE_BASE ** (-jnp.arange(nf, dtype=f32) / nf)
    ar = row[:, None] * freqs
    ac = col[:, None] * freqs
    ang = jnp.concatenate([ar, ar, ac, ac], axis=-1)
    sign = jnp.tile(jnp.concatenate([-jnp.ones((nf,), f32), jnp.ones((nf,), f32)]), 2)
    cos = jnp.tile(jnp.cos(ang), (1, 2))
    sin_signed = jnp.tile(jnp.sin(ang) * sign, (1, 2))
    return cos, sin_signed


def kernel(x, c, ctx, c_ctx, w_mod, b_mod, g_ffn1, ffn1_w_in, ffn1_w_out, g_mix, w_in,
           lambda_q1, lambda_k1, lambda_q2, lambda_k2, g_sub, w_pool, pool_scale, w_out,
           g_ffn2, ffn2_w_in, ffn2_w_out, g_final):
    b, n, d = x.shape
    l = 0
    cond = jnp.zeros((MOD_ROWS, d), f32).at[:b].set(c).at[b].set(c_ctx)
    mod = _mod_call(cond, w_mod[l], b_mod[l]).reshape(MOD_ROWS, N_MOD, d)
    lat_row = lambda bb: bb
    ctx_row = lambda bb: b

    w2_in, w2_out = ffn2_w_in[l].astype(bf16), ffn2_w_out[l].astype(bf16)
    w_in_b = w_in[l].astype(bf16)
    w_o_b = w_out[l].astype(bf16)
    w_pool_b = w_pool[l].astype(bf16)

    x1 = _ffn_call(x, mod, lat_row, 0, g_ffn1[l], ffn1_w_in[l], ffn1_w_out[l])
    cx1 = _ffn_call(ctx, mod, ctx_row, 0, g_ffn1[l], ffn1_w_in[l], ffn1_w_out[l])

    qt, k_lat, vt_lat, u = _inproj_call(x1, mod, lat_row, g_mix[l], w_in_b, _rope_tables(n))
    k_ctx, vt_ctx = _inproj_call(cx1, mod, ctx_row, g_mix[l],
                                 w_in_b[:, ATT_WIDTH:3 * ATT_WIDTH])

    lam_rows = jnp.stack([lambda_q1[l], lambda_k1[l], lambda_q2[l], lambda_k2[l]]).astype(f32)
    att = _attn_call(lam_rows, g_sub[l], qt, k_lat, k_ctx, vt_lat, vt_ctx)

    return _mixffn_call(x1, mod, att, u, w_pool_b, pool_scale[l], w_o_b,
                        g_ffn2[l], w2_in, w2_out, g_final)
```

```python
import functools
import math

import jax
import jax.numpy as jnp
from jax import lax
from jax.experimental import pallas as pl
from jax.experimental.pallas import tpu as pltpu

D_MODEL = 1024
GRID_W = 64
ATT_WIDTH = 512
POOL_WIDTH = 512
ATT_HEADS = 4
HEAD_DIM = 128
QK_DIM = 64
POOL_WINDOWS = (2, 4, 8, 16)
POOL_GROUP_DIM = 128
assert POOL_WINDOWS == tuple(2 << g for g in range(len(POOL_WINDOWS)))
D_FF = 2816
N_MOD = 9
ROPE_BASE = 10000.0
EPS = 1e-6
LAMBDA_INIT = 0.8 - 0.6 * math.exp(-0.3 * 0)

MXU_COLS = 256
SLABS_PER_TILE = MXU_COLS // HEAD_DIM
MOD_ROWS = 16
POOL_HALO = 8
NEG_BIG = -1e30
Q_SCALE = (QK_DIM ** -0.5) * math.log2(math.e)

VMEM_LIMIT = 56 * 1024 * 1024

TM_FFN = 512
TM_PROJ = 512
TM_MIX = 512
TQ_ATT = 1024
TK_ATT = 1024
MID_BLOCK = 2
QT_ATT = 2
FF_CHUNKS = (512, 512, 512, 512, 512, 256)

bf16 = jnp.bfloat16
f32 = jnp.float32


def _cparams(sem):
    return pltpu.CompilerParams(dimension_semantics=sem, vmem_limit_bytes=VMEM_LIMIT)


def _norm_mod(x, g, shift, scale):
    ms = jnp.mean(x * x, axis=-1, keepdims=True)
    y = x * lax.rsqrt(ms + EPS) * g
    return y * (1.0 + scale) + shift


def _mod_kernel(cond_ref, w_ref, b_ref, o_ref):
    cnd = cond_ref[...]
    a = (cnd * jax.nn.sigmoid(cnd)).astype(bf16)
    o_ref[...] = jnp.dot(a, w_ref[...].astype(bf16), preferred_element_type=f32) + b_ref[...]


def _mod_call(cond, w_mod, b_mod):
    n_out = w_mod.shape[1]
    tn = D_MODEL
    return pl.pallas_call(
        _mod_kernel,
        grid=(n_out // tn,),
        in_specs=[
            pl.BlockSpec((MOD_ROWS, D_MODEL), lambda j: (0, 0)),
            pl.BlockSpec((D_MODEL, tn), lambda j: (0, j)),
            pl.BlockSpec((1, tn), lambda j: (0, j)),
        ],
        out_specs=pl.BlockSpec((MOD_ROWS, tn), lambda j: (0, j)),
        out_shape=jax.ShapeDtypeStruct((MOD_ROWS, n_out), f32),
        compiler_params=_cparams(("arbitrary",)),
        name="mod",
    )(cond, w_mod, b_mod.reshape(1, n_out))


def _bf16(w):
    return w if w.dtype == bf16 else w.astype(bf16)


def _swiglu_rows(x, g, shift, scale, gate, win_ref, wout_ref):
    h = _norm_mod(x, g, shift, scale).astype(bf16)
    acc = None
    off = 0
    for tf in FF_CHUNKS:
        gt = jnp.dot(h, _bf16(win_ref[:, off:off + tf]), preferred_element_type=f32)
        up = jnp.dot(h, _bf16(win_ref[:, D_FF + off:D_FF + off + tf]),
                     preferred_element_type=f32)
        a = (gt * jax.nn.sigmoid(gt) * up).astype(bf16)
        part = jnp.dot(a, _bf16(wout_ref[off:off + tf, :]), preferred_element_type=f32)
        acc = part if acc is None else acc + part
        off += tf
    return x + 0.5 * gate * acc


def _ffn_kernel(x_ref, mod_ref, g_ref, win_ref, wout_ref, o_ref, *, k0):
    o_ref[0] = _swiglu_rows(x_ref[0], g_ref[...], mod_ref[0, k0:k0 + 1, :],
                            mod_ref[0, k0 + 1:k0 + 2, :], mod_ref[0, k0 + 2:k0 + 3, :],
                            win_ref, wout_ref)


def _ffn_call(x, mod, row_of_batch, k0, g, w_in, w_out):
    bx, t, d = x.shape
    tm = min(TM_FFN, t)
    return pl.pallas_call(
        functools.partial(_ffn_kernel, k0=k0),
        grid=(bx, t // tm),
        in_specs=[
            pl.BlockSpec((1, tm, d), lambda b, i: (b, i, 0)),
            pl.BlockSpec((1, N_MOD, d), lambda b, i: (row_of_batch(b), 0, 0)),
            pl.BlockSpec((1, d), lambda b, i: (0, 0)),
            pl.BlockSpec(w_in.shape, lambda b, i: (0, 0), pipeline_mode=pl.Buffered(1)),
            pl.BlockSpec(w_out.shape, lambda b, i: (0, 0), pipeline_mode=pl.Buffered(1)),
        ],
        out_specs=pl.BlockSpec((1, tm, d), lambda b, i: (b, i, 0)),
        out_shape=jax.ShapeDtypeStruct(x.shape, f32),
        compiler_params=_cparams(("parallel", "parallel")),
        name="ffn",
    )(x, mod, g.reshape(1, d), w_in, w_out)


def _rope(t, cos, sin_signed, first_half):
    w = t.shape[-1]
    partner = jnp.where(first_half, pltpu.roll(t, w - 16, axis=1), pltpu.roll(t, 16, axis=1))
    return t * cos + partner * sin_signed


def _inproj_kernel(x_ref, mod_ref, g_ref, w_ref, *rest, latent):
    if latent:
        cos_ref, sin_ref, qt_ref, k_ref, vt_ref, u_ref = rest
    else:
        k_ref, vt_ref = rest
    x = x_ref[0]
    shift = mod_ref[0, 3:4, :]
    scale = mod_ref[0, 4:5, :]
    h = _norm_mod(x, g_ref[...], shift, scale).astype(bf16)
    hx = jnp.dot(h, w_ref[...], preferred_element_type=f32)
    tm = x.shape[0]
    if latent:
        cos = jnp.concatenate([cos_ref[...]] * ATT_HEADS, axis=1)
        sin_signed = jnp.concatenate([sin_ref[...]] * ATT_HEADS, axis=1)
        lane_w = lax.broadcasted_iota(jnp.int32, (tm, ATT_WIDTH), 1)
        first_half = (lane_w % 32) < 16
        q = _rope(hx[:, :ATT_WIDTH], cos, sin_signed, first_half) * Q_SCALE
        k = _rope(hx[:, ATT_WIDTH:2 * ATT_WIDTH], cos, sin_signed, first_half)
        v = hx[:, 2 * ATT_WIDTH:3 * ATT_WIDTH]
        u_ref[0] = hx[:, 3 * ATT_WIDTH:]
    else:
        k = hx[:, :ATT_WIDTH]
        v = hx[:, ATT_WIDTH:]
    for hh in range(ATT_HEADS):
        hs = slice(hh * HEAD_DIM, (hh + 1) * HEAD_DIM)
        k_ref[0, hh] = k[:, hs].astype(bf16)
        vt_ref[0, hh] = v[:, hs].T.astype(bf16)
        if latent:
            qt_ref[0, hh] = q[:, hs].T.astype(bf16)


def _inproj_call(x, mod, row_of_batch, g, w, rope_tabs=None):
    bx, t, d = x.shape
    tm = min(TM_PROJ, t)
    latent = rope_tabs is not None
    in_specs = [
        pl.BlockSpec((1, tm, d), lambda b, i: (b, i, 0)),
        pl.BlockSpec((1, N_MOD, d), lambda b, i: (row_of_batch(b), 0, 0)),
        pl.BlockSpec((1, d), lambda b, i: (0, 0)),
        pl.BlockSpec(w.shape, lambda b, i: (0, 0), pipeline_mode=pl.Buffered(1)),
    ]
    args = [x, mod, g.reshape(1, d), w]
    k_spec = pl.BlockSpec((1, ATT_HEADS, tm, HEAD_DIM), lambda b, i: (b, 0, i, 0))
    k_shape = jax.ShapeDtypeStruct((bx, ATT_HEADS, t, HEAD_DIM), bf16)
    vt_spec = pl.BlockSpec((1, ATT_HEADS, HEAD_DIM, tm), lambda b, i: (b, 0, 0, i))
    vt_shape = jax.ShapeDtypeStruct((bx, ATT_HEADS, HEAD_DIM, t), bf16)
    if latent:
        in_specs += [pl.BlockSpec((tm, HEAD_DIM), lambda b, i: (i, 0))] * 2
        args += list(rope_tabs)
        out_specs = [pl.BlockSpec((1, ATT_HEADS, HEAD_DIM, tm), lambda b, i: (b, 0, 0, i)),
                     k_spec, vt_spec,
                     pl.BlockSpec((1, tm, POOL_WIDTH), lambda b, i: (b, i, 0))]
        out_shape = [jax.ShapeDtypeStruct((bx, ATT_HEADS, HEAD_DIM, t), bf16), k_shape, vt_shape,
                     jax.ShapeDtypeStruct((bx, t, POOL_WIDTH), f32)]
    else:
        out_specs = [k_spec, vt_spec]
        out_shape = [k_shape, vt_shape]
    return pl.pallas_call(
        functools.partial(_inproj_kernel, latent=latent),
        grid=(bx, t // tm),
        in_specs=in_specs,
        out_specs=out_specs,
        out_shape=out_shape,
        compiler_params=_cparams(("parallel", "parallel")),
        name="inproj_lat" if latent else "inproj_ctx",
    )(*args)


def _attn_kernel(lam_ref, gsub_ref, qt_ref, kl_ref, kc_ref, vtl_ref, vtc_ref, o_ref,
                 s_scr, cm_scr, m_scr, l_scr, acc_scr):
    tq = s_scr.shape[2] * HEAD_DIM
    n_lat = kl_ref.shape[2]
    m_scr[...] = jnp.full(m_scr.shape, NEG_BIG, f32)
    l_scr[...] = jnp.zeros(l_scr.shape, f32)
    acc_scr[...] = jnp.zeros(acc_scr.shape, f32)

    n_col = tq // MXU_COLS
    n_chunks = n_lat // TK_ATT
    stages = [(ti, j) for ti in range(QT_ATT) for j in range(n_chunks + 1)]

    def scores(s, mi, t):
        ti, j = stages[s]
        keys = keys_of(j)
        tk = keys.shape[0]
        q_tile = qt_ref[0, 0, :, ti * tq + t * MXU_COLS:ti * tq + (t + 1) * MXU_COLS]
        blank = jnp.zeros((QK_DIM, MXU_COLS), bf16)
        q_tile = jnp.concatenate([q_tile[0:QK_DIM], blank] if mi == 0
                                 else [blank, q_tile[QK_DIM:]], axis=0)
        st = jnp.dot(keys, q_tile, preferred_element_type=f32)
        for c in range(SLABS_PER_TILE):
            s_scr[s % 2, mi, t * SLABS_PER_TILE + c, 0:tk, :] = (
                st[:, c * HEAD_DIM:(c + 1) * HEAD_DIM])
        cm_scr[s % 2, mi, :, t * MXU_COLS:(t + 1) * MXU_COLS] = (
            jnp.max(st.reshape(tk // 8, 8, MXU_COLS), axis=0))

    def softmax_pv(s, mi, t, mid=None):
        ti, j = stages[s]
        slot = s % 2
        vt = values_of(j)
        tk = vt.shape[1]
        ts = slice(t * MXU_COLS, (t + 1) * MXU_COLS)
        m_old = m_scr[ti, mi, :, ts]
        m_new = jnp.maximum(m_old, jnp.max(cm_scr[slot, mi, :, ts], axis=0, keepdims=True))
        m_scr[ti, mi, :, ts] = m_new
        alpha = jnp.exp2(m_old - m_new)
        mb = [jnp.broadcast_to(m_new[:, c * HEAD_DIM:(c + 1) * HEAD_DIM], (MXU_COLS, HEAD_DIM))
              for c in range(SLABS_PER_TILE)]
        pv = None
        psum = None
        n_blocks = tk // MXU_COLS
        for bi in range(n_blocks):
            rs = slice(bi * MXU_COLS, (bi + 1) * MXU_COLS)
            e = [jnp.exp2(s_scr[slot, mi, t * SLABS_PER_TILE + c, rs, :] - mb[c])
                 for c in range(SLABS_PER_TILE)]
            p = jnp.concatenate([ec.astype(bf16) for ec in e], axis=1)
            part = jnp.dot(vt[:, rs], p, preferred_element_type=f32)
            pv = part if pv is None else pv + part
            part = jnp.concatenate(
                [jnp.sum(ec.reshape(MXU_COLS // 8, 8, HEAD_DIM), axis=0) for ec in e], axis=1)
            psum = part if psum is None else psum + part
            if bi == min(MID_BLOCK, n_blocks - 1) and mid is not None:
                mid()
        acc_scr[ti, mi, :, ts] = alpha * acc_scr[ti, mi, :, ts] + pv
        l_scr[ti, mi, :, ts] = alpha * l_scr[ti, mi, :, ts] + psum

    def keys_of(j):
        if j < n_chunks:
            return kl_ref[0, 0, j * TK_ATT:(j + 1) * TK_ATT, :]
        return kc_ref[0, 0]

    def values_of(j):
        if j < n_chunks:
            return vtl_ref[0, 0, :, j * TK_ATT:(j + 1) * TK_ATT]
        return vtc_ref[0, 0]

    def finalize(ti):
        lam_v = lam_ref[...]
        lam = (jnp.exp(jnp.sum(lam_v[0:1] * lam_v[1:2], axis=-1, keepdims=True))
               - jnp.exp(jnp.sum(lam_v[2:3] * lam_v[3:4], axis=-1, keepdims=True))
               + LAMBDA_INIT)
        o1 = acc_scr[ti, 0] / jnp.sum(l_scr[ti, 0], axis=0, keepdims=True)
        o2 = acc_scr[ti, 1] / jnp.sum(l_scr[ti, 1], axis=0, keepdims=True)
        o = o1 - lam * o2
        ms = jnp.mean(o * o, axis=0, keepdims=True)
        o = o * lax.rsqrt(ms + EPS) * (gsub_ref[...] * (1.0 - LAMBDA_INIT))
        o_ref[0, ti * tq:(ti + 1) * tq, :] = o.T.astype(bf16)

    for mi in range(2):
        for t in range(n_col):
            scores(0, mi, t)
    for s in range(len(stages)):
        for mi in range(2):
            for t in range(n_col):
                nxt = None
                if s + 1 < len(stages):
                    nxt = functools.partial(scores, s + 1, mi, t)
                softmax_pv(s, mi, t, nxt)
        if stages[s][1] == n_chunks:
            finalize(stages[s][0])


def _attn_call(lam_rows, g_sub, qt, k_lat, k_ctx, vt_lat, vt_ctx):
    b, _, _, n = qt.shape
    n_ctx = k_ctx.shape[2]
    tq = TQ_ATT
    tstep = QT_ATT * tq
    return pl.pallas_call(
        _attn_kernel,
        grid=(b, ATT_HEADS, n // tstep),
        in_specs=[
            pl.BlockSpec((4, QK_DIM), lambda bb, h, i: (0, 0)),
            pl.BlockSpec((HEAD_DIM, 1), lambda bb, h, i: (0, 0)),
            pl.BlockSpec((1, 1, HEAD_DIM, tstep), lambda bb, h, i: (bb, h, 0, i)),
            pl.BlockSpec((1, 1, n, HEAD_DIM), lambda bb, h, i: (bb, h, 0, 0)),
            pl.BlockSpec((1, 1, n_ctx, HEAD_DIM), lambda bb, h, i: (bb, h, 0, 0)),
            pl.BlockSpec((1, 1, HEAD_DIM, n), lambda bb, h, i: (bb, h, 0, 0)),
            pl.BlockSpec((1, 1, HEAD_DIM, n_ctx), lambda bb, h, i: (bb, h, 0, 0)),
        ],
        out_specs=pl.BlockSpec((1, tstep, HEAD_DIM), lambda bb, h, i: (bb, i, h)),
        out_shape=jax.ShapeDtypeStruct((b, n, ATT_WIDTH), bf16),
        scratch_shapes=[
            pltpu.VMEM((2, 2, tq // HEAD_DIM, TK_ATT, HEAD_DIM), f32),
            pltpu.VMEM((2, 2, 8, tq), f32),
            pltpu.VMEM((QT_ATT, 2, 1, tq), f32),
            pltpu.VMEM((QT_ATT, 2, 8, tq), f32),
            pltpu.VMEM((QT_ATT, 2, HEAD_DIM, tq), f32),
        ],
        compiler_params=_cparams(("parallel", "parallel", "arbitrary")),
        name="attn",
    )(lam_rows, g_sub.reshape(HEAD_DIM, 1), qt, k_lat, k_ctx, vt_lat, vt_ctx)


def _mixffn_kernel(x_ref, mod_ref, att_ref, u_ref, up_ref, un_ref, wp_ref, ps_ref, wo_ref,
                   g2_ref, win_ref, wout_ref, gf_ref, o_ref, run, *, n_tok):
    i = pl.program_id(1)
    nt = pl.num_programs(1)
    tm = x_ref.shape[1]
    hl = POOL_HALO
    y_att = jnp.dot(att_ref[0], wo_ref[0:ATT_WIDTH, :], preferred_element_type=f32)
    base = 2 * hl
    top = base + tm + hl
    zeros = jnp.zeros((hl, POOL_WIDTH), f32)
    run[0, 0:hl, :] = zeros
    run[0, hl:base, :] = jnp.where(i > 0, up_ref[0], 0.0)
    run[0, base:base + tm, :] = u_ref[0]
    run[0, base + tm:top, :] = jnp.where(i < nt - 1, un_ref[0], 0.0)
    for k in range(1, len(POOL_WINDOWS)):
        reach = 1 << (k - 1)
        lanes = slice((k - 1) * POOL_GROUP_DIM, POOL_WIDTH)
        run[k, 0:hl, :] = zeros
        run[k, hl:top, lanes] = run[k - 1, hl - reach:top - reach, lanes] + run[k - 1, hl:top, lanes]

    t_head = i * tm + lax.broadcasted_iota(jnp.int32, (hl, 1), 0)
    t_tail = t_head + (tm - hl)
    pooled = []
    for gi, w in enumerate(POOL_WINDOWS):
        lo = w // 2
        hi = w - lo - 1
        sl = slice(gi * POOL_GROUP_DIM, (gi + 1) * POOL_GROUP_DIM)
        if gi + 1 < len(POOL_WINDOWS):
            total = run[gi + 1, base + hi:base + hi + tm, sl]
        else:
            half = w // 2
            total = (run[gi, base + hi - half:base + hi - half + tm, sl]
                     + run[gi, base + hi:base + hi + tm, sl])

        def cnt(t):
            return (jnp.minimum(t + hi, n_tok - 1) - jnp.maximum(t - lo, 0) + 1).astype(f32)

        mean = jnp.concatenate([total[0:hl] / cnt(t_head),
                                total[hl:tm - hl] * (1.0 / w),
                                total[tm - hl:tm] / cnt(t_tail)], axis=0)
        diff = (mean - run[0, base:base + tm, sl]).astype(bf16)
        pooled.append(jnp.dot(diff, wp_ref[gi], preferred_element_type=f32))
    pool = (jnp.concatenate(pooled, axis=-1) * ps_ref[...]).astype(bf16)
    y = y_att + jnp.dot(pool, wo_ref[ATT_WIDTH:, :], preferred_element_type=f32)
    x2 = x_ref[0] + mod_ref[0, 5:6, :] * y
    y = _swiglu_rows(x2, g2_ref[...], mod_ref[0, 6:7, :], mod_ref[0, 7:8, :], mod_ref[0, 8:9, :],
                     win_ref, wout_ref)
    ms = jnp.mean(y * y, axis=-1, keepdims=True)
    o_ref[0] = y * lax.rsqrt(ms + EPS) * gf_ref[...]


def _mixffn_call(x, mod, att, u, w_pool, pool_scale, w_out, g2, w2_in, w2_out, g_final):
    b, n, d = x.shape
    tm = TM_MIX
    hb = tm // POOL_HALO
    last_halo_block = n // POOL_HALO - 1
    const2 = lambda bb, i: (0, 0)
    return pl.pallas_call(
        functools.partial(_mixffn_kernel, n_tok=n),
        grid=(b, n // tm),
        in_specs=[
            pl.BlockSpec((1, tm, d), lambda bb, i: (bb, i, 0)),
            pl.BlockSpec((1, N_MOD, d), lambda bb, i: (bb, 0, 0)),
            pl.BlockSpec((1, tm, ATT_WIDTH), lambda bb, i: (bb, i, 0)),
            pl.BlockSpec((1, tm, POOL_WIDTH), lambda bb, i: (bb, i, 0)),
            pl.BlockSpec((1, POOL_HALO, POOL_WIDTH),
                         lambda bb, i: (bb, jnp.maximum(i * hb - 1, 0), 0)),
            pl.BlockSpec((1, POOL_HALO, POOL_WIDTH),
                         lambda bb, i: (bb, jnp.minimum((i + 1) * hb, last_halo_block), 0)),
            pl.BlockSpec(w_pool.shape, lambda bb, i: (0, 0, 0), pipeline_mode=pl.Buffered(1)),
            pl.BlockSpec((1, POOL_WIDTH), const2),
            pl.BlockSpec(w_out.shape, const2, pipeline_mode=pl.Buffered(1)),
            pl.BlockSpec((1, d), const2),
            pl.BlockSpec(w2_in.shape, const2, pipeline_mode=pl.Buffered(1)),
            pl.BlockSpec(w2_out.shape, const2, pipeline_mode=pl.Buffered(1)),
            pl.BlockSpec((1, d), const2),
        ],
        out_specs=pl.BlockSpec((1, tm, d), lambda bb, i: (bb, i, 0)),
        out_shape=jax.ShapeDtypeStruct(x.shape, f32),
        scratch_shapes=[pltpu.VMEM((len(POOL_WINDOWS), tm + 3 * POOL_HALO, POOL_WIDTH), f32)],
        compiler_params=_cparams(("parallel", "parallel")),
        name="mixffn",
    )(x, mod, att, u, u, u, w_pool, pool_scale.reshape(1, POOL_WIDTH), w_out,
      g2.reshape(1, d), w2_in, w2_out, g_final.reshape(1, d))


def _rope_tables(n):
    rows = n // GRID_W
    row = jnp.repeat(jnp.arange(rows, dtype=f32), GRID_W)
    col = jnp.tile(jnp.arange(GRID_W, dtype=f32), rows)
    nf = QK_DIM // 4
    freqs = ROPE_BASE ** (-jnp.arange(nf, dtype=f32) / nf)
    ar = row[:, None] * freqs
    ac = col[:, None] * freqs
    ang = jnp.concatenate([ar, ar, ac, ac], axis=-1)
    sign = jnp.tile(jnp.concatenate([-jnp.ones((nf,), f32), jnp.ones((nf,), f32)]), 2)
    cos = jnp.tile(jnp.cos(ang), (1, 2))
    sin_signed = jnp.tile(jnp.sin(ang) * sign, (1, 2))
    return cos, sin_signed


def kernel(x, c, ctx, c_ctx, w_mod, b_mod, g_ffn1, ffn1_w_in, ffn1_w_out, g_mix, w_in,
           lambda_q1, lambda_k1, lambda_q2, lambda_k2, g_sub, w_pool, pool_scale, w_out,
           g_ffn2, ffn2_w_in, ffn2_w_out, g_final):
    b, n, d = x.shape
    l = 0
    cond = jnp.zeros((MOD_ROWS, d), f32).at[:b].set(c).at[b].set(c_ctx)
    mod = _mod_call(cond, w_mod[l], b_mod[l]).reshape(MOD_ROWS, N_MOD, d)
    lat_row = lambda bb: bb
    ctx_row = lambda bb: b

    w2_in, w2_out = ffn2_w_in[l].astype(bf16), ffn2_w_out[l].astype(bf16)
    w_in_b = w_in[l].astype(bf16)
    w_o_b = w_out[l].astype(bf16)
    w_pool_b = w_pool[l].astype(bf16)

    x1 = _ffn_call(x, mod, lat_row, 0, g_ffn1[l], ffn1_w_in[l], ffn1_w_out[l])
    cx1 = _ffn_call(ctx, mod, ctx_row, 0, g_ffn1[l], ffn1_w_in[l], ffn1_w_out[l])

    qt, k_lat, vt_lat, u = _inproj_call(x1, mod, lat_row, g_mix[l], w_in_b, _rope_tables(n))
    k_ctx, vt_ctx = _inproj_call(cx1, mod, ctx_row, g_mix[l],
                                 w_in_b[:, ATT_WIDTH:3 * ATT_WIDTH])

    lam_rows = jnp.stack([lambda_q1[l], lambda_k1[l], lambda_q2[l], lambda_k2[l]]).astype(f32)
    att = _attn_call(lam_rows, g_sub[l], qt, k_lat, k_ctx, vt_lat, vt_ctx)

    return _mixffn_call(x1, mod, att, u, w_pool_b, pool_scale[l], w_o_b,
                        g_ffn2[l], w2_in, w2_out, g_final)
```

```python
import functools
import math

import jax
import jax.numpy as jnp
from jax import lax
from jax.experimental import pallas as pl
from jax.experimental.pallas import tpu as pltpu

D_MODEL = 1024
GRID_W = 64
ATT_WIDTH = 512
POOL_WIDTH = 512
ATT_HEADS = 4
HEAD_DIM = 128
QK_DIM = 64
POOL_WINDOWS = (2, 4, 8, 16)
POOL_GROUP_DIM = 128
assert POOL_WINDOWS == tuple(2 << g for g in range(len(POOL_WINDOWS)))
D_FF = 2816
N_MOD = 9
ROPE_BASE = 10000.0
EPS = 1e-6
LAMBDA_INIT = 0.8 - 0.6 * math.exp(-0.3 * 0)

MXU_COLS = 256
SLABS_PER_TILE = MXU_COLS // HEAD_DIM
MOD_ROWS = 16
POOL_HALO = 8
NEG_BIG = -1e30
Q_SCALE = (QK_DIM ** -0.5) * math.log2(math.e)

VMEM_LIMIT = 56 * 1024 * 1024

TM_FFN = 512
TM_PROJ = 512
TM_MIX = 512
TQ_ATT = 1024
TK_ATT = 1024
MID_BLOCK = 2
QT_ATT = 4
FF_CHUNKS = (512, 512, 512, 512, 512, 256)

bf16 = jnp.bfloat16
f32 = jnp.float32


def _cparams(sem):
    return pltpu.CompilerParams(dimension_semantics=sem, vmem_limit_bytes=VMEM_LIMIT)


def _norm_mod(x, g, shift, scale):
    ms = jnp.mean(x * x, axis=-1, keepdims=True)
    y = x * lax.rsqrt(ms + EPS) * g
    return y * (1.0 + scale) + shift


def _mod_kernel(cond_ref, w_ref, b_ref, o_ref):
    cnd = cond_ref[...]
    a = (cnd * jax.nn.sigmoid(cnd)).astype(bf16)
    o_ref[...] = jnp.dot(a, w_ref[...].astype(bf16), preferred_element_type=f32) + b_ref[...]


def _mod_call(cond, w_mod, b_mod):
    n_out = w_mod.shape[1]
    tn = D_MODEL
    return pl.pallas_call(
        _mod_kernel,
        grid=(n_out // tn,),
        in_specs=[
            pl.BlockSpec((MOD_ROWS, D_MODEL), lambda j: (0, 0)),
            pl.BlockSpec((D_MODEL, tn), lambda j: (0, j)),
            pl.BlockSpec((1, tn), lambda j: (0, j)),
        ],
        out_specs=pl.BlockSpec((MOD_ROWS, tn), lambda j: (0, j)),
        out_shape=jax.ShapeDtypeStruct((MOD_ROWS, n_out), f32),
        compiler_params=_cparams(("arbitrary",)),
        name="mod",
    )(cond, w_mod, b_mod.reshape(1, n_out))


def _bf16(w):
    return w if w.dtype == bf16 else w.astype(bf16)


def _swiglu_rows(x, g, shift, scale, gate, win_ref, wout_ref):
    h = _norm_mod(x, g, shift, scale).astype(bf16)
    acc = None
    off = 0
    for tf in FF_CHUNKS:
        gt = jnp.dot(h, _bf16(win_ref[:, off:off + tf]), preferred_element_type=f32)
        up = jnp.dot(h, _bf16(win_ref[:, D_FF + off:D_FF + off + tf]),
                     preferred_element_type=f32)
        a = (gt * jax.nn.sigmoid(gt) * up).astype(bf16)
        part = jnp.dot(a, _bf16(wout_ref[off:off + tf, :]), preferred_element_type=f32)
        acc = part if acc is None else acc + part
        off += tf
    return x + 0.5 * gate * acc


def _ffn_kernel(x_ref, mod_ref, g_ref, win_ref, wout_ref, o_ref, *, k0):
    o_ref[0] = _swiglu_rows(x_ref[0], g_ref[...], mod_ref[0, k0:k0 + 1, :],
                            mod_ref[0, k0 + 1:k0 + 2, :], mod_ref[0, k0 + 2:k0 + 3, :],
                            win_ref, wout_ref)


def _ffn_call(x, mod, row_of_batch, k0, g, w_in, w_out):
    bx, t, d = x.shape
    tm = min(TM_FFN, t)
    return pl.pallas_call(
        functools.partial(_ffn_kernel, k0=k0),
        grid=(bx, t // tm),
        in_specs=[
            pl.BlockSpec((1, tm, d), lambda b, i: (b, i, 0)),
            pl.BlockSpec((1, N_MOD, d), lambda b, i: (row_of_batch(b), 0, 0)),
            pl.BlockSpec((1, d), lambda b, i: (0, 0)),
            pl.BlockSpec(w_in.shape, lambda b, i: (0, 0), pipeline_mode=pl.Buffered(1)),
            pl.BlockSpec(w_out.shape, lambda b, i: (0, 0), pipeline_mode=pl.Buffered(1)),
        ],
        out_specs=pl.BlockSpec((1, tm, d), lambda b, i: (b, i, 0)),
        out_shape=jax.ShapeDtypeStruct(x.shape, f32),
        compiler_params=_cparams(("parallel", "parallel")),
        name="ffn",
    )(x, mod, g.reshape(1, d), w_in, w_out)


def _rope(t, cos, sin_signed, first_half):
    w = t.shape[-1]
    partner = jnp.where(first_half, pltpu.roll(t, w - 16, axis=1), pltpu.roll(t, 16, axis=1))
    return t * cos + partner * sin_signed


def _inproj_kernel(x_ref, mod_ref, g_ref, w_ref, *rest, latent):
    if latent:
        cos_ref, sin_ref, qt_ref, k_ref, vt_ref, u_ref = rest
    else:
        k_ref, vt_ref = rest
    x = x_ref[0]
    shift = mod_ref[0, 3:4, :]
    scale = mod_ref[0, 4:5, :]
    h = _norm_mod(x, g_ref[...], shift, scale).astype(bf16)
    hx = jnp.dot(h, w_ref[...], preferred_element_type=f32)
    tm = x.shape[0]
    if latent:
        cos = jnp.concatenate([cos_ref[...]] * ATT_HEADS, axis=1)
        sin_signed = jnp.concatenate([sin_ref[...]] * ATT_HEADS, axis=1)
        lane_w = lax.broadcasted_iota(jnp.int32, (tm, ATT_WIDTH), 1)
        first_half = (lane_w % 32) < 16
        q = _rope(hx[:, :ATT_WIDTH], cos, sin_signed, first_half) * Q_SCALE
        k = _rope(hx[:, ATT_WIDTH:2 * ATT_WIDTH], cos, sin_signed, first_half)
        v = hx[:, 2 * ATT_WIDTH:3 * ATT_WIDTH]
        u_ref[0] = hx[:, 3 * ATT_WIDTH:]
    else:
        k = hx[:, :ATT_WIDTH]
        v = hx[:, ATT_WIDTH:]
    for hh in range(ATT_HEADS):
        hs = slice(hh * HEAD_DIM, (hh + 1) * HEAD_DIM)
        k_ref[0, hh] = k[:, hs].astype(bf16)
        vt_ref[0, hh] = v[:, hs].T.astype(bf16)
        if latent:
            qt_ref[0, hh] = q[:, hs].T.astype(bf16)


def _inproj_call(x, mod, row_of_batch, g, w, rope_tabs=None):
    bx, t, d = x.shape
    tm = min(TM_PROJ, t)
    latent = rope_tabs is not None
    in_specs = [
        pl.BlockSpec((1, tm, d), lambda b, i: (b, i, 0)),
        pl.BlockSpec((1, N_MOD, d), lambda b, i: (row_of_batch(b), 0, 0)),
        pl.BlockSpec((1, d), lambda b, i: (0, 0)),
        pl.BlockSpec(w.shape, lambda b, i: (0, 0), pipeline_mode=pl.Buffered(1)),
    ]
    args = [x, mod, g.reshape(1, d), w]
    k_spec = pl.BlockSpec((1, ATT_HEADS, tm, HEAD_DIM), lambda b, i: (b, 0, i, 0))
    k_shape = jax.ShapeDtypeStruct((bx, ATT_HEADS, t, HEAD_DIM), bf16)
    vt_spec = pl.BlockSpec((1, ATT_HEADS, HEAD_DIM, tm), lambda b, i: (b, 0, 0, i))
    vt_shape = jax.ShapeDtypeStruct((bx, ATT_HEADS, HEAD_DIM, t), bf16)
    if latent:
        in_specs += [pl.BlockSpec((tm, HEAD_DIM), lambda b, i: (i, 0))] * 2
        args += list(rope_tabs)
        out_specs = [pl.BlockSpec((1, ATT_HEADS, HEAD_DIM, tm), lambda b, i: (b, 0, 0, i)),
                     k_spec, vt_spec,
                     pl.BlockSpec((1, tm, POOL_WIDTH), lambda b, i: (b, i, 0))]
        out_shape = [jax.ShapeDtypeStruct((bx, ATT_HEADS, HEAD_DIM, t), bf16), k_shape, vt_shape,
                     jax.ShapeDtypeStruct((bx, t, POOL_WIDTH), f32)]
    else:
        out_specs = [k_spec, vt_spec]
        out_shape = [k_shape, vt_shape]
    return pl.pallas_call(
        functools.partial(_inproj_kernel, latent=latent),
        grid=(bx, t // tm),
        in_specs=in_specs,
        out_specs=out_specs,
        out_shape=out_shape,
        compiler_params=_cparams(("parallel", "parallel")),
        name="inproj_lat" if latent else "inproj_ctx",
    )(*args)


def _attn_kernel(lam_ref, gsub_ref, qt_ref, kl_ref, kc_ref, vtl_ref, vtc_ref, o_ref,
                 s_scr, cm_scr, m_scr, l_scr, acc_scr):
    tq = s_scr.shape[2] * HEAD_DIM
    n_lat = kl_ref.shape[2]
    m_scr[...] = jnp.full(m_scr.shape, NEG_BIG, f32)
    l_scr[...] = jnp.zeros(l_scr.shape, f32)
    acc_scr[...] = jnp.zeros(acc_scr.shape, f32)

    n_col = tq // MXU_COLS
    n_chunks = n_lat // TK_ATT
    stages = [(ti, j) for ti in range(QT_ATT) for j in range(n_chunks + 1)]

    def scores(s, mi, t):
        ti, j = stages[s]
        keys = keys_of(j)
        tk = keys.shape[0]
        q_tile = qt_ref[0, 0, :, ti * tq + t * MXU_COLS:ti * tq + (t + 1) * MXU_COLS]
        blank = jnp.zeros((QK_DIM, MXU_COLS), bf16)
        q_tile = jnp.concatenate([q_tile[0:QK_DIM], blank] if mi == 0
                                 else [blank, q_tile[QK_DIM:]], axis=0)
        st = jnp.dot(keys, q_tile, preferred_element_type=f32)
        for c in range(SLABS_PER_TILE):
            s_scr[s % 2, mi, t * SLABS_PER_TILE + c, 0:tk, :] = (
                st[:, c * HEAD_DIM:(c + 1) * HEAD_DIM])
        cm_scr[s % 2, mi, :, t * MXU_COLS:(t + 1) * MXU_COLS] = (
            jnp.max(st.reshape(tk // 8, 8, MXU_COLS), axis=0))

    def softmax_pv(s, mi, t, mid=None):
        ti, j = stages[s]
        slot = s % 2
        vt = values_of(j)
        tk = vt.shape[1]
        ts = slice(t * MXU_COLS, (t + 1) * MXU_COLS)
        m_old = m_scr[ti, mi, :, ts]
        m_new = jnp.maximum(m_old, jnp.max(cm_scr[slot, mi, :, ts], axis=0, keepdims=True))
        m_scr[ti, mi, :, ts] = m_new
        alpha = jnp.exp2(m_old - m_new)
        mb = [jnp.broadcast_to(m_new[:, c * HEAD_DIM:(c + 1) * HEAD_DIM], (MXU_COLS, HEAD_DIM))
              for c in range(SLABS_PER_TILE)]
        pv = None
        psum = None
        n_blocks = tk // MXU_COLS
        for bi in range(n_blocks):
            rs = slice(bi * MXU_COLS, (bi + 1) * MXU_COLS)
            e = [jnp.exp2(s_scr[slot, mi, t * SLABS_PER_TILE + c, rs, :] - mb[c])
                 for c in range(SLABS_PER_TILE)]
            p = jnp.concatenate([ec.astype(bf16) for ec in e], axis=1)
            part = jnp.dot(vt[:, rs], p, preferred_element_type=f32)
            pv = part if pv is None else pv + part
            part = jnp.concatenate(
                [jnp.sum(ec.reshape(MXU_COLS // 8, 8, HEAD_DIM), axis=0) for ec in e], axis=1)
            psum = part if psum is None else psum + part
            if bi == min(MID_BLOCK, n_blocks - 1) and mid is not None:
                mid()
        acc_scr[ti, mi, :, ts] = alpha * acc_scr[ti, mi, :, ts] + pv
        l_scr[ti, mi, :, ts] = alpha * l_scr[ti, mi, :, ts] + psum

    def keys_of(j):
        if j < n_chunks:
            return kl_ref[0, 0, j * TK_ATT:(j + 1) * TK_ATT, :]
        return kc_ref[0, 0]

    def values_of(j):
        if j < n_chunks:
            return vtl_ref[0, 0, :, j * TK_ATT:(j + 1) * TK_ATT]
        return vtc_ref[0, 0]

    def finalize(ti):
        lam_v = lam_ref[...]
        lam = (jnp.exp(jnp.sum(lam_v[0:1] * lam_v[1:2], axis=-1, keepdims=True))
               - jnp.exp(jnp.sum(lam_v[2:3] * lam_v[3:4], axis=-1, keepdims=True))
               + LAMBDA_INIT)
        o1 = acc_scr[ti, 0] / jnp.sum(l_scr[ti, 0], axis=0, keepdims=True)
        o2 = acc_scr[ti, 1] / jnp.sum(l_scr[ti, 1], axis=0, keepdims=True)
        o = o1 - lam * o2
        ms = jnp.mean(o * o, axis=0, keepdims=True)
        o = o * lax.rsqrt(ms + EPS) * (gsub_ref[...] * (1.0 - LAMBDA_INIT))
        o_ref[0, ti * tq:(ti + 1) * tq, :] = o.T.astype(bf16)

    for mi in range(2):
        for t in range(n_col):
            scores(0, mi, t)
    for s in range(len(stages)):
        for mi in range(2):
            for t in range(n_col):
                nxt = None
                if s + 1 < len(stages):
                    nxt = functools.partial(scores, s + 1, mi, t)
                softmax_pv(s, mi, t, nxt)
        if stages[s][1] == n_chunks:
            finalize(stages[s][0])


def _attn_call(lam_rows, g_sub, qt, k_lat, k_ctx, vt_lat, vt_ctx):
    b, _, _, n = qt.shape
    n_ctx = k_ctx.shape[2]
    tq = TQ_ATT
    tstep = QT_ATT * tq
    return pl.pallas_call(
        _attn_kernel,
        grid=(b, ATT_HEADS, n // tstep),
        in_specs=[
            pl.BlockSpec((4, QK_DIM), lambda bb, h, i: (0, 0)),
            pl.BlockSpec((HEAD_DIM, 1), lambda bb, h, i: (0, 0)),
            pl.BlockSpec((1, 1, HEAD_DIM, tstep), lambda bb, h, i: (bb, h, 0, i)),
            pl.BlockSpec((1, 1, n, HEAD_DIM), lambda bb, h, i: (bb, h, 0, 0)),
            pl.BlockSpec((1, 1, n_ctx, HEAD_DIM), lambda bb, h, i: (bb, h, 0, 0)),
            pl.BlockSpec((1, 1, HEAD_DIM, n), lambda bb, h, i: (bb, h, 0, 0)),
            pl.BlockSpec((1, 1, HEAD_DIM, n_ctx), lambda bb, h, i: (bb, h, 0, 0)),
        ],
        out_specs=pl.BlockSpec((1, tstep, HEAD_DIM), lambda bb, h, i: (bb, i, h)),
        out_shape=jax.ShapeDtypeStruct((b, n, ATT_WIDTH), bf16),
        scratch_shapes=[
            pltpu.VMEM((2, 2, tq // HEAD_DIM, TK_ATT, HEAD_DIM), f32),
            pltpu.VMEM((2, 2, 8, tq), f32),
            pltpu.VMEM((QT_ATT, 2, 1, tq), f32),
            pltpu.VMEM((QT_ATT, 2, 8, tq), f32),
            pltpu.VMEM((QT_ATT, 2, HEAD_DIM, tq), f32),
        ],
        compiler_params=_cparams(("parallel", "parallel", "arbitrary")),
        name="attn",
    )(lam_rows, g_sub.reshape(HEAD_DIM, 1), qt, k_lat, k_ctx, vt_lat, vt_ctx)


def _mixffn_kernel(x_ref, mod_ref, att_ref, u_ref, up_ref, un_ref, wp_ref, ps_ref, wo_ref,
                   g2_ref, win_ref, wout_ref, gf_ref, o_ref, run, *, n_tok):
    i = pl.program_id(1)
    nt = pl.num_programs(1)
    tm = x_ref.shape[1]
    hl = POOL_HALO
    y_att = jnp.dot(att_ref[0], wo_ref[0:ATT_WIDTH, :], preferred_element_type=f32)
    base = 2 * hl
    top = base + tm + hl
    zeros = jnp.zeros((hl, POOL_WIDTH), f32)
    run[0, 0:hl, :] = zeros
    run[0, hl:base, :] = jnp.where(i > 0, up_ref[0], 0.0)
    run[0, base:base + tm, :] = u_ref[0]
    run[0, base + tm:top, :] = jnp.where(i < nt - 1, un_ref[0], 0.0)
    for k in range(1, len(POOL_WINDOWS)):
        reach = 1 << (k - 1)
        lanes = slice((k - 1) * POOL_GROUP_DIM, POOL_WIDTH)
        run[k, 0:hl, :] = zeros
        run[k, hl:top, lanes] = run[k - 1, hl - reach:top - reach, lanes] + run[k - 1, hl:top, lanes]

    t_head = i * tm + lax.broadcasted_iota(jnp.int32, (hl, 1), 0)
    t_tail = t_head + (tm - hl)
    pooled = []
    for gi, w in enumerate(POOL_WINDOWS):
        lo = w // 2
        hi = w - lo - 1
        sl = slice(gi * POOL_GROUP_DIM, (gi + 1) * POOL_GROUP_DIM)
        if gi + 1 < len(POOL_WINDOWS):
            total = run[gi + 1, base + hi:base + hi + tm, sl]
        else:
            half = w // 2
            total = (run[gi, base + hi - half:base + hi - half + tm, sl]
                     + run[gi, base + hi:base + hi + tm, sl])

        def cnt(t):
            return (jnp.minimum(t + hi, n_tok - 1) - jnp.maximum(t - lo, 0) + 1).astype(f32)

        mean = jnp.concatenate([total[0:hl] / cnt(t_head),
                                total[hl:tm - hl] * (1.0 / w),
                                total[tm - hl:tm] / cnt(t_tail)], axis=0)
        diff = (mean - run[0, base:base + tm, sl]).astype(bf16)
        pooled.append(jnp.dot(diff, wp_ref[gi], preferred_element_type=f32))
    pool = (jnp.concatenate(pooled, axis=-1) * ps_ref[...]).astype(bf16)
    y = y_att + jnp.dot(pool, wo_ref[ATT_WIDTH:, :], preferred_element_type=f32)
    x2 = x_ref[0] + mod_ref[0, 5:6, :] * y
    y = _swiglu_rows(x2, g2_ref[...], mod_ref[0, 6:7, :], mod_ref[0, 7:8, :], mod_ref[0, 8:9, :],
                     win_ref, wout_ref)
    ms = jnp.mean(y * y, axis=-1, keepdims=True)
    o_ref[0] = y * lax.rsqrt(ms + EPS) * gf_ref[...]


def _mixffn_call(x, mod, att, u, w_pool, pool_scale, w_out, g2, w2_in, w2_out, g_final):
    b, n, d = x.shape
    tm = TM_MIX
    hb = tm // POOL_HALO
    last_halo_block = n // POOL_HALO - 1
    const2 = lambda bb, i: (0, 0)
    return pl.pallas_call(
        functools.partial(_mixffn_kernel, n_tok=n),
        grid=(b, n // tm),
        in_specs=[
            pl.BlockSpec((1, tm, d), lambda bb, i: (bb, i, 0)),
            pl.BlockSpec((1, N_MOD, d), lambda bb, i: (bb, 0, 0)),
            pl.BlockSpec((1, tm, ATT_WIDTH), lambda bb, i: (bb, i, 0)),
            pl.BlockSpec((1, tm, POOL_WIDTH), lambda bb, i: (bb, i, 0)),
            pl.BlockSpec((1, POOL_HALO, POOL_WIDTH),
                         lambda bb, i: (bb, jnp.maximum(i * hb - 1, 0), 0)),
            pl.BlockSpec((1, POOL_HALO, POOL_WIDTH),
                         lambda bb, i: (bb, jnp.minimum((i + 1) * hb, last_halo_block), 0)),
            pl.BlockSpec(w_pool.shape, lambda bb, i: (0, 0, 0), pipeline_mode=pl.Buffered(1)),
            pl.BlockSpec((1, POOL_WIDTH), const2),
            pl.BlockSpec(w_out.shape, const2, pipeline_mode=pl.Buffered(1)),
            pl.BlockSpec((1, d), const2),
            pl.BlockSpec(w2_in.shape, const2, pipeline_mode=pl.Buffered(1)),
            pl.BlockSpec(w2_out.shape, const2, pipeline_mode=pl.Buffered(1)),
            pl.BlockSpec((1, d), const2),
        ],
        out_specs=pl.BlockSpec((1, tm, d), lambda bb, i: (bb, i, 0)),
        out_shape=jax.ShapeDtypeStruct(x.shape, f32),
        scratch_shapes=[pltpu.VMEM((len(POOL_WINDOWS), tm + 3 * POOL_HALO, POOL_WIDTH), f32)],
        compiler_params=_cparams(("parallel", "parallel")),
        name="mixffn",
    )(x, mod, att, u, u, u, w_pool, pool_scale.reshape(1, POOL_WIDTH), w_out,
      g2.reshape(1, d), w2_in, w2_out, g_final.reshape(1, d))


def _rope_tables(n):
    rows = n // GRID_W
    row = jnp.repeat(jnp.arange(rows, dtype=f32), GRID_W)
    col = jnp.tile(jnp.arange(GRID_W, dtype=f32), rows)
    nf = QK_DIM // 4
    freqs = ROPE_BASE ** (-jnp.arange(nf, dtype=f32) / nf)
    ar = row[:, None] * freqs
    ac = col[:, None] * freqs
    ang = jnp.concatenate([ar, ar, ac, ac], axis=-1)
    sign = jnp.tile(jnp.concatenate([-jnp.ones((nf,), f32), jnp.ones((nf,), f32)]), 2)
    cos = jnp.tile(jnp.cos(ang), (1, 2))
    sin_signed = jnp.tile(jnp.sin(ang) * sign, (1, 2))
    return cos, sin_signed


def kernel(x, c, ctx, c_ctx, w_mod, b_mod, g_ffn1, ffn1_w_in, ffn1_w_out, g_mix, w_in,
           lambda_q1, lambda_k1, lambda_q2, lambda_k2, g_sub, w_pool, pool_scale, w_out,
           g_ffn2, ffn2_w_in, ffn2_w_out, g_final):
    b, n, d = x.shape
    l = 0
    cond = jnp.zeros((MOD_ROWS, d), f32).at[:b].set(c).at[b].set(c_ctx)
    mod = _mod_call(cond, w_mod[l], b_mod[l]).reshape(MOD_ROWS, N_MOD, d)
    lat_row = lambda bb: bb
    ctx_row = lambda bb: b

    w2_in, w2_out = ffn2_w_in[l].astype(bf16), ffn2_w_out[l].astype(bf16)
    w_in_b = w_in[l].astype(bf16)
    w_o_b = w_out[l].astype(bf16)
    w_pool_b = w_pool[l].astype(bf16)

    x1 = _ffn_call(x, mod, lat_row, 0, g_ffn1[l], ffn1_w_in[l], ffn1_w_out[l])
    cx1 = _ffn_call(ctx, mod, ctx_row, 0, g_ffn1[l], ffn1_w_in[l], ffn1_w_out[l])

    qt, k_lat, vt_lat, u = _inproj_call(x1, mod, lat_row, g_mix[l], w_in_b, _rope_tables(n))
    k_ctx, vt_ctx = _inproj_call(cx1, mod, ctx_row, g_mix[l],
                                 w_in_b[:, ATT_WIDTH:3 * ATT_WIDTH])

    lam_rows = jnp.stack([lambda_q1[l], lambda_k1[l], lambda_q2[l], lambda_k2[l]]).astype(f32)
    att = _attn_call(lam_rows, g_sub[l], qt, k_lat, k_ctx, vt_lat, vt_ctx)

    return _mixffn_call(x1, mod, att, u, w_pool_b, pool_scale[l], w_o_b,
                        g_ffn2[l], w2_in, w2_out, g_final)
```

```python
import functools
import math

import jax
import jax.numpy as jnp
from jax import lax
from jax.experimental import pallas as pl
from jax.experimental.pallas import tpu as pltpu

D_MODEL = 1024
GRID_W = 64
ATT_WIDTH = 512
POOL_WIDTH = 512
ATT_HEADS = 4
HEAD_DIM = 128
QK_DIM = 64
POOL_WINDOWS = (2, 4, 8, 16)
POOL_GROUP_DIM = 128
assert POOL_WINDOWS == tuple(2 << g for g in range(len(POOL_WINDOWS)))
D_FF = 2816
N_MOD = 9
ROPE_BASE = 10000.0
EPS = 1e-6
LAMBDA_INIT = 0.8 - 0.6 * math.exp(-0.3 * 0)

MXU_COLS = 256
SLABS_PER_TILE = MXU_COLS // HEAD_DIM
MOD_ROWS = 16
POOL_HALO = 8
NEG_BIG = -1e30
Q_SCALE = (QK_DIM ** -0.5) * math.log2(math.e)

VMEM_LIMIT = 56 * 1024 * 1024

TM_FFN = 512
TM_PROJ = 512
TM_MIX = 512
TQ_ATT = 1024
TK_ATT = 1024
MID_BLOCK = 2
QT_ATT = 1
FF_CHUNKS = (512, 512, 512, 512, 512, 256)

bf16 = jnp.bfloat16
f32 = jnp.float32


def _cparams(sem):
    return pltpu.CompilerParams(dimension_semantics=sem, vmem_limit_bytes=VMEM_LIMIT)


def _norm_mod(x, g, shift, scale):
    ms = jnp.mean(x * x, axis=-1, keepdims=True)
    y = x * lax.rsqrt(ms + EPS) * g
    return y * (1.0 + scale) + shift


def _mod_kernel(cond_ref, w_ref, b_ref, o_ref):
    cnd = cond_ref[...]
    a = (cnd * jax.nn.sigmoid(cnd)).astype(bf16)
    o_ref[...] = jnp.dot(a, w_ref[...].astype(bf16), preferred_element_type=f32) + b_ref[...]


def _mod_call(cond, w_mod, b_mod):
    n_out = w_mod.shape[1]
    tn = D_MODEL
    return pl.pallas_call(
        _mod_kernel,
        grid=(n_out // tn,),
        in_specs=[
            pl.BlockSpec((MOD_ROWS, D_MODEL), lambda j: (0, 0)),
            pl.BlockSpec((D_MODEL, tn), lambda j: (0, j)),
            pl.BlockSpec((1, tn), lambda j: (0, j)),
        ],
        out_specs=pl.BlockSpec((MOD_ROWS, tn), lambda j: (0, j)),
        out_shape=jax.ShapeDtypeStruct((MOD_ROWS, n_out), f32),
        compiler_params=_cparams(("arbitrary",)),
        name="mod",
    )(cond, w_mod, b_mod.reshape(1, n_out))


def _bf16(w):
    return w if w.dtype == bf16 else w.astype(bf16)


def _swiglu_rows(x, g, shift, scale, gate, win_ref, wout_ref):
    h = _norm_mod(x, g, shift, scale).astype(bf16)
    acc = None
    off = 0
    for tf in FF_CHUNKS:
        gt = jnp.dot(h, _bf16(win_ref[:, off:off + tf]), preferred_element_type=f32)
        up = jnp.dot(h, _bf16(win_ref[:, D_FF + off:D_FF + off + tf]),
                     preferred_element_type=f32)
        a = (gt * jax.nn.sigmoid(gt) * up).astype(bf16)
        part = jnp.dot(a, _bf16(wout_ref[off:off + tf, :]), preferred_element_type=f32)
        acc = part if acc is None else acc + part
        off += tf
    return x + 0.5 * gate * acc


def _ffn_kernel(x_ref, mod_ref, g_ref, win_ref, wout_ref, o_ref, *, k0):
    o_ref[0] = _swiglu_rows(x_ref[0], g_ref[...], mod_ref[0, k0:k0 + 1, :],
                            mod_ref[0, k0 + 1:k0 + 2, :], mod_ref[0, k0 + 2:k0 + 3, :],
                            win_ref, wout_ref)


def _ffn_call(x, mod, row_of_batch, k0, g, w_in, w_out):
    bx, t, d = x.shape
    tm = min(TM_FFN, t)
    return pl.pallas_call(
        functools.partial(_ffn_kernel, k0=k0),
        grid=(bx, t // tm),
        in_specs=[
            pl.BlockSpec((1, tm, d), lambda b, i: (b, i, 0)),
            pl.BlockSpec((1, N_MOD, d), lambda b, i: (row_of_batch(b), 0, 0)),
            pl.BlockSpec((1, d), lambda b, i: (0, 0)),
            pl.BlockSpec(w_in.shape, lambda b, i: (0, 0), pipeline_mode=pl.Buffered(1)),
            pl.BlockSpec(w_out.shape, lambda b, i: (0, 0), pipeline_mode=pl.Buffered(1)),
        ],
        out_specs=pl.BlockSpec((1, tm, d), lambda b, i: (b, i, 0)),
        out_shape=jax.ShapeDtypeStruct(x.shape, f32),
        compiler_params=_cparams(("parallel", "parallel")),
        name="ffn",
    )(x, mod, g.reshape(1, d), w_in, w_out)


def _rope(t, cos, sin_signed, first_half):
    w = t.shape[-1]
    partner = jnp.where(first_half, pltpu.roll(t, w - 16, axis=1), pltpu.roll(t, 16, axis=1))
    return t * cos + partner * sin_signed


def _inproj_kernel(x_ref, mod_ref, g_ref, w_ref, *rest, latent):
    if latent:
        cos_ref, sin_ref, qt_ref, k_ref, vt_ref, u_ref = rest
    else:
        k_ref, vt_ref = rest
    x = x_ref[0]
    shift = mod_ref[0, 3:4, :]
    scale = mod_ref[0, 4:5, :]
    h = _norm_mod(x, g_ref[...], shift, scale).astype(bf16)
    hx = jnp.dot(h, w_ref[...], preferred_element_type=f32)
    tm = x.shape[0]
    if latent:
        cos = jnp.concatenate([cos_ref[...]] * ATT_HEADS, axis=1)
        sin_signed = jnp.concatenate([sin_ref[...]] * ATT_HEADS, axis=1)
        lane_w = lax.broadcasted_iota(jnp.int32, (tm, ATT_WIDTH), 1)
        first_half = (lane_w % 32) < 16
        q = _rope(hx[:, :ATT_WIDTH], cos, sin_signed, first_half) * Q_SCALE
        k = _rope(hx[:, ATT_WIDTH:2 * ATT_WIDTH], cos, sin_signed, first_half)
        v = hx[:, 2 * ATT_WIDTH:3 * ATT_WIDTH]
        u_ref[0] = hx[:, 3 * ATT_WIDTH:]
    else:
        k = hx[:, :ATT_WIDTH]
        v = hx[:, ATT_WIDTH:]
    for hh in range(ATT_HEADS):
        hs = slice(hh * HEAD_DIM, (hh + 1) * HEAD_DIM)
        k_ref[0, hh] = k[:, hs].astype(bf16)
        vt_ref[0, hh] = v[:, hs].T.astype(bf16)
        if latent:
            qt_ref[0, hh] = q[:, hs].T.astype(bf16)


def _inproj_call(x, mod, row_of_batch, g, w, rope_tabs=None):
    bx, t, d = x.shape
    tm = min(TM_PROJ, t)
    latent = rope_tabs is not None
    in_specs = [
        pl.BlockSpec((1, tm, d), lambda b, i: (b, i, 0)),
        pl.BlockSpec((1, N_MOD, d), lambda b, i: (row_of_batch(b), 0, 0)),
        pl.BlockSpec((1, d), lambda b, i: (0, 0)),
        pl.BlockSpec(w.shape, lambda b, i: (0, 0), pipeline_mode=pl.Buffered(1)),
    ]
    args = [x, mod, g.reshape(1, d), w]
    k_spec = pl.BlockSpec((1, ATT_HEADS, tm, HEAD_DIM), lambda b, i: (b, 0, i, 0))
    k_shape = jax.ShapeDtypeStruct((bx, ATT_HEADS, t, HEAD_DIM), bf16)
    vt_spec = pl.BlockSpec((1, ATT_HEADS, HEAD_DIM, tm), lambda b, i: (b, 0, 0, i))
    vt_shape = jax.ShapeDtypeStruct((bx, ATT_HEADS, HEAD_DIM, t), bf16)
    if latent:
        in_specs += [pl.BlockSpec((tm, HEAD_DIM), lambda b, i: (i, 0))] * 2
        args += list(rope_tabs)
        out_specs = [pl.BlockSpec((1, ATT_HEADS, HEAD_DIM, tm), lambda b, i: (b, 0, 0, i)),
                     k_spec, vt_spec,
                     pl.BlockSpec((1, tm, POOL_WIDTH), lambda b, i: (b, i, 0))]
        out_shape = [jax.ShapeDtypeStruct((bx, ATT_HEADS, HEAD_DIM, t), bf16), k_shape, vt_shape,
                     jax.ShapeDtypeStruct((bx, t, POOL_WIDTH), f32)]
    else:
        out_specs = [k_spec, vt_spec]
        out_shape = [k_shape, vt_shape]
    return pl.pallas_call(
        functools.partial(_inproj_kernel, latent=latent),
        grid=(bx, t // tm),
        in_specs=in_specs,
        out_specs=out_specs,
        out_shape=out_shape,
        compiler_params=_cparams(("parallel", "parallel")),
        name="inproj_lat" if latent else "inproj_ctx",
    )(*args)


def _attn_kernel(lam_ref, gsub_ref, qt_ref, kl_ref, kc_ref, vtl_ref, vtc_ref, o_ref,
                 s_scr, cm_scr, m_scr, l_scr, acc_scr):
    tq = s_scr.shape[2] * HEAD_DIM
    n_lat = kl_ref.shape[2]
    m_scr[...] = jnp.full(m_scr.shape, NEG_BIG, f32)
    l_scr[...] = jnp.zeros(l_scr.shape, f32)
    acc_scr[...] = jnp.zeros(acc_scr.shape, f32)

    n_col = tq // MXU_COLS
    n_chunks = n_lat // TK_ATT
    stages = [(ti, j) for ti in range(QT_ATT) for j in range(n_chunks + 1)]

    def scores(s, mi, t):
        ti, j = stages[s]
        keys = keys_of(j)
        tk = keys.shape[0]
        q_tile = qt_ref[0, 0, :, ti * tq + t * MXU_COLS:ti * tq + (t + 1) * MXU_COLS]
        blank = jnp.zeros((QK_DIM, MXU_COLS), bf16)
        q_tile = jnp.concatenate([q_tile[0:QK_DIM], blank] if mi == 0
                                 else [blank, q_tile[QK_DIM:]], axis=0)
        st = jnp.dot(keys, q_tile, preferred_element_type=f32)
        for c in range(SLABS_PER_TILE):
            s_scr[s % 2, mi, t * SLABS_PER_TILE + c, 0:tk, :] = (
                st[:, c * HEAD_DIM:(c + 1) * HEAD_DIM])
        cm_scr[s % 2, mi, :, t * MXU_COLS:(t + 1) * MXU_COLS] = (
            jnp.max(st.reshape(tk // 8, 8, MXU_COLS), axis=0))

    def softmax_pv(s, mi, t, mid=None):
        ti, j = stages[s]
        slot = s % 2
        vt = values_of(j)
        tk = vt.shape[1]
        ts = slice(t * MXU_COLS, (t + 1) * MXU_COLS)
        m_old = m_scr[ti, mi, :, ts]
        m_new = jnp.maximum(m_old, jnp.max(cm_scr[slot, mi, :, ts], axis=0, keepdims=True))
        m_scr[ti, mi, :, ts] = m_new
        alpha = jnp.exp2(m_old - m_new)
        mb = [jnp.broadcast_to(m_new[:, c * HEAD_DIM:(c + 1) * HEAD_DIM], (MXU_COLS, HEAD_DIM))
              for c in range(SLABS_PER_TILE)]
        pv = None
        psum = None
        n_blocks = tk // MXU_COLS
        for bi in range(n_blocks):
            rs = slice(bi * MXU_COLS, (bi + 1) * MXU_COLS)
            e = [jnp.exp2(s_scr[slot, mi, t * SLABS_PER_TILE + c, rs, :] - mb[c])
                 for c in range(SLABS_PER_TILE)]
            p = jnp.concatenate([ec.astype(bf16) for ec in e], axis=1)
            part = jnp.dot(vt[:, rs], p, preferred_element_type=f32)
            pv = part if pv is None else pv + part
            part = jnp.concatenate(
                [jnp.sum(ec.reshape(MXU_COLS // 8, 8, HEAD_DIM), axis=0) for ec in e], axis=1)
            psum = part if psum is None else psum + part
            if bi == min(MID_BLOCK, n_blocks - 1) and mid is not None:
                mid()
        acc_scr[ti, mi, :, ts] = alpha * acc_scr[ti, mi, :, ts] + pv
        l_scr[ti, mi, :, ts] = alpha * l_scr[ti, mi, :, ts] + psum

    def keys_of(j):
        if j < n_chunks:
            return kl_ref[0, 0, j * TK_ATT:(j + 1) * TK_ATT, :]
        return kc_ref[0, 0]

    def values_of(j):
        if j < n_chunks:
            return vtl_ref[0, 0, :, j * TK_ATT:(j + 1) * TK_ATT]
        return vtc_ref[0, 0]

    def finalize(ti):
        lam_v = lam_ref[...]
        lam = (jnp.exp(jnp.sum(lam_v[0:1] * lam_v[1:2], axis=-1, keepdims=True))
               - jnp.exp(jnp.sum(lam_v[2:3] * lam_v[3:4], axis=-1, keepdims=True))
               + LAMBDA_INIT)
        o1 = acc_scr[ti, 0] / jnp.sum(l_scr[ti, 0], axis=0, keepdims=True)
        o2 = acc_scr[ti, 1] / jnp.sum(l_scr[ti, 1], axis=0, keepdims=True)
        o = o1 - lam * o2
        ms = jnp.mean(o * o, axis=0, keepdims=True)
        o = o * lax.rsqrt(ms + EPS) * (gsub_ref[...] * (1.0 - LAMBDA_INIT))
        o_ref[0, ti * tq:(ti + 1) * tq, :] = o.T.astype(bf16)

    for mi in range(2):
        for t in range(n_col):
            scores(0, mi, t)
    for s in range(len(stages)):
        for mi in range(2):
            for t in range(n_col):
                nxt = None
                if s + 1 < len(stages):
                    nxt = functools.partial(scores, s + 1, mi, t)
                softmax_pv(s, mi, t, nxt)
        if stages[s][1] == n_chunks:
            finalize(stages[s][0])


def _attn_call(lam_rows, g_sub, qt, k_lat, k_ctx, vt_lat, vt_ctx):
    b, _, _, n = qt.shape
    n_ctx = k_ctx.shape[2]
    tq = TQ_ATT
    tstep = QT_ATT * tq
    return pl.pallas_call(
        _attn_kernel,
        grid=(b, ATT_HEADS, n // tstep),
        in_specs=[
            pl.BlockSpec((4, QK_DIM), lambda bb, h, i: (0, 0)),
            pl.BlockSpec((HEAD_DIM, 1), lambda bb, h, i: (0, 0)),
            pl.BlockSpec((1, 1, HEAD_DIM, tstep), lambda bb, h, i: (bb, h, 0, i)),
            pl.BlockSpec((1, 1, n, HEAD_DIM), lambda bb, h, i: (bb, h, 0, 0)),
            pl.BlockSpec((1, 1, n_ctx, HEAD_DIM), lambda bb, h, i: (bb, h, 0, 0)),
            pl.BlockSpec((1, 1, HEAD_DIM, n), lambda bb, h, i: (bb, h, 0, 0)),
            pl.BlockSpec((1, 1, HEAD_DIM, n_ctx), lambda bb, h, i: (bb, h, 0, 0)),
        ],
        out_specs=pl.BlockSpec((1, tstep, HEAD_DIM), lambda bb, h, i: (bb, i, h)),
        out_shape=jax.ShapeDtypeStruct((b, n, ATT_WIDTH), bf16),
        scratch_shapes=[
            pltpu.VMEM((2, 2, tq // HEAD_DIM, TK_ATT, HEAD_DIM), f32),
            pltpu.VMEM((2, 2, 8, tq), f32),
            pltpu.VMEM((QT_ATT, 2, 1, tq), f32),
            pltpu.VMEM((QT_ATT, 2, 8, tq), f32),
            pltpu.VMEM((QT_ATT, 2, HEAD_DIM, tq), f32),
        ],
        compiler_params=_cparams(("parallel", "parallel", "arbitrary")),
        name="attn",
    )(lam_rows, g_sub.reshape(HEAD_DIM, 1), qt, k_lat, k_ctx, vt_lat, vt_ctx)


def _mixffn_kernel(x_ref, mod_ref, att_ref, u_ref, up_ref, un_ref, wp_ref, ps_ref, wo_ref,
                   g2_ref, win_ref, wout_ref, gf_ref, o_ref, run, *, n_tok):
    i = pl.program_id(1)
    nt = pl.num_programs(1)
    tm = x_ref.shape[1]
    hl = POOL_HALO
    y_att = jnp.dot(att_ref[0], wo_ref[0:ATT_WIDTH, :], preferred_element_type=f32)
    base = 2 * hl
    top = base + tm + hl
    zeros = jnp.zeros((hl, POOL_WIDTH), f32)
    run[0, 0:hl, :] = zeros
    run[0, hl:base, :] = jnp.where(i > 0, up_ref[0], 0.0)
    run[0, base:base + tm, :] = u_ref[0]
    run[0, base + tm:top, :] = jnp.where(i < nt - 1, un_ref[0], 0.0)
    for k in range(1, len(POOL_WINDOWS)):
        reach = 1 << (k - 1)
        lanes = slice((k - 1) * POOL_GROUP_DIM, POOL_WIDTH)
        run[k, 0:hl, :] = zeros
        run[k, hl:top, lanes] = run[k - 1, hl - reach:top - reach, lanes] + run[k - 1, hl:top, lanes]

    t_head = i * tm + lax.broadcasted_iota(jnp.int32, (hl, 1), 0)
    t_tail = t_head + (tm - hl)
    pooled = []
    for gi, w in enumerate(POOL_WINDOWS):
        lo = w // 2
        hi = w - lo - 1
        sl = slice(gi * POOL_GROUP_DIM, (gi + 1) * POOL_GROUP_DIM)
        if gi + 1 < len(POOL_WINDOWS):
            total = run[gi + 1, base + hi:base + hi + tm, sl]
        else:
            half = w // 2
            total = (run[gi, base + hi - half:base + hi - half + tm, sl]
                     + run[gi, base + hi:base + hi + tm, sl])

        def cnt(t):
            return (jnp.minimum(t + hi, n_tok - 1) - jnp.maximum(t - lo, 0) + 1).astype(f32)

        mean = jnp.concatenate([total[0:hl] / cnt(t_head),
                                total[hl:tm - hl] * (1.0 / w),
                                total[tm - hl:tm] / cnt(t_tail)], axis=0)
        diff = (mean - run[0, base:base + tm, sl]).astype(bf16)
        pooled.append(jnp.dot(diff, wp_ref[gi], preferred_element_type=f32))
    pool = (jnp.concatenate(pooled, axis=-1) * ps_ref[...]).astype(bf16)
    y = y_att + jnp.dot(pool, wo_ref[ATT_WIDTH:, :], preferred_element_type=f32)
    x2 = x_ref[0] + mod_ref[0, 5:6, :] * y
    y = _swiglu_rows(x2, g2_ref[...], mod_ref[0, 6:7, :], mod_ref[0, 7:8, :], mod_ref[0, 8:9, :],
                     win_ref, wout_ref)
    ms = jnp.mean(y * y, axis=-1, keepdims=True)
    o_ref[0] = y * lax.rsqrt(ms + EPS) * gf_ref[...]


def _mixffn_call(x, mod, att, u, w_pool, pool_scale, w_out, g2, w2_in, w2_out, g_final):
    b, n, d = x.shape
    tm = TM_MIX
    hb = tm // POOL_HALO
    last_halo_block = n // POOL_HALO - 1
    const2 = lambda bb, i: (0, 0)
    return pl.pallas_call(
        functools.partial(_mixffn_kernel, n_tok=n),
        grid=(b, n // tm),
        in_specs=[
            pl.BlockSpec((1, tm, d), lambda bb, i: (bb, i, 0)),
            pl.BlockSpec((1, N_MOD, d), lambda bb, i: (bb, 0, 0)),
            pl.BlockSpec((1, tm, ATT_WIDTH), lambda bb, i: (bb, i, 0)),
            pl.BlockSpec((1, tm, POOL_WIDTH), lambda bb, i: (bb, i, 0)),
            pl.BlockSpec((1, POOL_HALO, POOL_WIDTH),
                         lambda bb, i: (bb, jnp.maximum(i * hb - 1, 0), 0)),
            pl.BlockSpec((1, POOL_HALO, POOL_WIDTH),
                         lambda bb, i: (bb, jnp.minimum((i + 1) * hb, last_halo_block), 0)),
            pl.BlockSpec(w_pool.shape, lambda bb, i: (0, 0, 0), pipeline_mode=pl.Buffered(1)),
            pl.BlockSpec((1, POOL_WIDTH), const2),
            pl.BlockSpec(w_out.shape, const2, pipeline_mode=pl.Buffered(1)),
            pl.BlockSpec((1, d), const2),
            pl.BlockSpec(w2_in.shape, const2, pipeline_mode=pl.Buffered(1)),
            pl.BlockSpec(w2_out.shape, const2, pipeline_mode=pl.Buffered(1)),
            pl.BlockSpec((1, d), const2),
        ],
        out_specs=pl.BlockSpec((1, tm, d), lambda bb, i: (bb, i, 0)),
        out_shape=jax.ShapeDtypeStruct(x.shape, f32),
        scratch_shapes=[pltpu.VMEM((len(POOL_WINDOWS), tm + 3 * POOL_HALO, POOL_WIDTH), f32)],
        compiler_params=_cparams(("parallel", "parallel")),
        name="mixffn",
    )(x, mod, att, u, u, u, w_pool, pool_scale.reshape(1, POOL_WIDTH), w_out,
      g2.reshape(1, d), w2_in, w2_out, g_final.reshape(1, d))


def _rope_tables(n):
    rows = n // GRID_W
    row = jnp.repeat(jnp.arange(rows, dtype=f32), GRID_W)
    col = jnp.tile(jnp.arange(GRID_W, dtype=f32), rows)
    nf = QK_DIM // 4
    freqs = ROPE_BASE ** (-jnp.arange(nf, dtype=f32) / nf)
    ar = row[:, None] * freqs
    ac = col[:, None] * freqs
    ang = jnp.concatenate([ar, ar, ac, ac], axis=-1)
    sign = jnp.tile(jnp.concatenate([-jnp.ones((nf,), f32), jnp.ones((nf,), f32)]), 2)
    cos = jnp.tile(jnp.cos(ang), (1, 2))
    sin_signed = jnp.tile(jnp.sin(ang) * sign, (1, 2))
    return cos, sin_signed


def kernel(x, c, ctx, c_ctx, w_mod, b_mod, g_ffn1, ffn1_w_in, ffn1_w_out, g_mix, w_in,
           lambda_q1, lambda_k1, lambda_q2, lambda_k2, g_sub, w_pool, pool_scale, w_out,
           g_ffn2, ffn2_w_in, ffn2_w_out, g_final):
    b, n, d = x.shape
    l = 0
    cond = jnp.zeros((MOD_ROWS, d), f32).at[:b].set(c).at[b].set(c_ctx)
    mod = _mod_call(cond, w_mod[l], b_mod[l]).reshape(MOD_ROWS, N_MOD, d)
    lat_row = lambda bb: bb
    ctx_row = lambda bb: b

    w2_in, w2_out = ffn2_w_in[l].astype(bf16), ffn2_w_out[l].astype(bf16)
    w_in_b = w_in[l].astype(bf16)
    w_o_b = w_out[l].astype(bf16)
    w_pool_b = w_pool[l].astype(bf16)

    x1 = _ffn_call(x, mod, lat_row, 0, g_ffn1[l], ffn1_w_in[l], ffn1_w_out[l])
    cx1 = _ffn_call(ctx, mod, ctx_row, 0, g_ffn1[l], ffn1_w_in[l], ffn1_w_out[l])

    qt, k_lat, vt_lat, u = _inproj_call(x1, mod, lat_row, g_mix[l], w_in_b, _rope_tables(n))
    k_ctx, vt_ctx = _inproj_call(cx1, mod, ctx_row, g_mix[l],
                                 w_in_b[:, ATT_WIDTH:3 * ATT_WIDTH])

    lam_rows = jnp.stack([lambda_q1[l], lambda_k1[l], lambda_q2[l], lambda_k2[l]]).astype(f32)
    att = _attn_call(lam_rows, g_sub[l], qt, k_lat, k_ctx, vt_lat, vt_ctx)

    return _mixffn_call(x1, mod, att, u, w_pool_b, pool_scale[l], w_o_b,
                        g_ffn2[l], w2_in, w2_out, g_final)
```

```python
import functools
import math

import jax
import jax.numpy as jnp
from jax import lax
from jax.experimental import pallas as pl
from jax.experimental.pallas import tpu as pltpu

D_MODEL = 1024
GRID_W = 64
ATT_WIDTH = 512
POOL_WIDTH = 512
ATT_HEADS = 4
HEAD_DIM = 128
QK_DIM = 64
POOL_WINDOWS = (2, 4, 8, 16)
POOL_GROUP_DIM = 128
assert POOL_WINDOWS == tuple(2 << g for g in range(len(POOL_WINDOWS)))
D_FF = 2816
N_MOD = 9
ROPE_BASE = 10000.0
EPS = 1e-6
LAMBDA_INIT = 0.8 - 0.6 * math.exp(-0.3 * 0)

MXU_COLS = 256
SLABS_PER_TILE = MXU_COLS // HEAD_DIM
MOD_ROWS = 16
POOL_HALO = 8
NEG_BIG = -1e30
Q_SCALE = (QK_DIM ** -0.5) * math.log2(math.e)

VMEM_LIMIT = 56 * 1024 * 1024

TM_FFN = 512
TM_PROJ = 512
TM_MIX = 512
TQ_ATT = 1024
TK_ATT = 1024
MID_BLOCK = 2
QT_ATT = 2
FF_CHUNKS = (512, 512, 512, 512, 512, 256)

bf16 = jnp.bfloat16
f32 = jnp.float32


def _cparams(sem):
    return pltpu.CompilerParams(dimension_semantics=sem, vmem_limit_bytes=VMEM_LIMIT)


def _norm_mod(x, g, shift, scale):
    ms = jnp.mean(x * x, axis=-1, keepdims=True)
    y = x * lax.rsqrt(ms + EPS) * g
    return y * (1.0 + scale) + shift


def _mod_kernel(cond_ref, w_ref, b_ref, o_ref):
    cnd = cond_ref[...]
    a = (cnd * jax.nn.sigmoid(cnd)).astype(bf16)
    o_ref[...] = jnp.dot(a, w_ref[...].astype(bf16), preferred_element_type=f32) + b_ref[...]


def _mod_call(cond, w_mod, b_mod):
    n_out = w_mod.shape[1]
    tn = D_MODEL
    return pl.pallas_call(
        _mod_kernel,
        grid=(n_out // tn,),
        in_specs=[
            pl.BlockSpec((MOD_ROWS, D_MODEL), lambda j: (0, 0)),
            pl.BlockSpec((D_MODEL, tn), lambda j: (0, j)),
            pl.BlockSpec((1, tn), lambda j: (0, j)),
        ],
        out_specs=pl.BlockSpec((MOD_ROWS, tn), lambda j: (0, j)),
        out_shape=jax.ShapeDtypeStruct((MOD_ROWS, n_out), f32),
        compiler_params=_cparams(("arbitrary",)),
        name="mod",
    )(cond, w_mod, b_mod.reshape(1, n_out))


def _bf16(w):
    return w if w.dtype == bf16 else w.astype(bf16)


def _swiglu_rows(x, g, shift, scale, gate, win_ref, wout_ref):
    h = _norm_mod(x, g, shift, scale).astype(bf16)
    acc = None
    off = 0
    for tf in FF_CHUNKS:
        gt = jnp.dot(h, _bf16(win_ref[:, off:off + tf]), preferred_element_type=f32)
        up = jnp.dot(h, _bf16(win_ref[:, D_FF + off:D_FF + off + tf]),
                     preferred_element_type=f32)
        a = (gt * jax.nn.sigmoid(gt) * up).astype(bf16)
        part = jnp.dot(a, _bf16(wout_ref[off:off + tf, :]), preferred_element_type=f32)
        acc = part if acc is None else acc + part
        off += tf
    return x + 0.5 * gate * acc


def _ffn_kernel(x_ref, mod_ref, g_ref, win_ref, wout_ref, o_ref, *, k0):
    o_ref[0] = _swiglu_rows(x_ref[0], g_ref[...], mod_ref[0, k0:k0 + 1, :],
                            mod_ref[0, k0 + 1:k0 + 2, :], mod_ref[0, k0 + 2:k0 + 3, :],
                            win_ref, wout_ref)


def _ffn_call(x, mod, row_of_batch, k0, g, w_in, w_out):
    bx, t, d = x.shape
    tm = min(TM_FFN, t)
    return pl.pallas_call(
        functools.partial(_ffn_kernel, k0=k0),
        grid=(bx, t // tm),
        in_specs=[
            pl.BlockSpec((1, tm, d), lambda b, i: (b, i, 0)),
            pl.BlockSpec((1, N_MOD, d), lambda b, i: (row_of_batch(b), 0, 0)),
            pl.BlockSpec((1, d), lambda b, i: (0, 0)),
            pl.BlockSpec(w_in.shape, lambda b, i: (0, 0), pipeline_mode=pl.Buffered(1)),
            pl.BlockSpec(w_out.shape, lambda b, i: (0, 0), pipeline_mode=pl.Buffered(1)),
        ],
        out_specs=pl.BlockSpec((1, tm, d), lambda b, i: (b, i, 0)),
        out_shape=jax.ShapeDtypeStruct(x.shape, f32),
        compiler_params=_cparams(("parallel", "parallel")),
        name="ffn",
    )(x, mod, g.reshape(1, d), w_in, w_out)


def _rope(t, cos, sin_signed, first_half):
    w = t.shape[-1]
    partner = jnp.where(first_half, pltpu.roll(t, w - 16, axis=1), pltpu.roll(t, 16, axis=1))
    return t * cos + partner * sin_signed


def _inproj_kernel(x_ref, mod_ref, g_ref, w_ref, *rest, latent):
    if latent:
        cos_ref, sin_ref, qt_ref, k_ref, vt_ref, u_ref = rest
    else:
        k_ref, vt_ref = rest
    x = x_ref[0]
    shift = mod_ref[0, 3:4, :]
    scale = mod_ref[0, 4:5, :]
    h = _norm_mod(x, g_ref[...], shift, scale).astype(bf16)
    hx = jnp.dot(h, w_ref[...], preferred_element_type=f32)
    tm = x.shape[0]
    if latent:
        cos = jnp.concatenate([cos_ref[...]] * ATT_HEADS, axis=1)
        sin_signed = jnp.concatenate([sin_ref[...]] * ATT_HEADS, axis=1)
        lane_w = lax.broadcasted_iota(jnp.int32, (tm, ATT_WIDTH), 1)
        first_half = (lane_w % 32) < 16
        q = _rope(hx[:, :ATT_WIDTH], cos, sin_signed, first_half) * Q_SCALE
        k = _rope(hx[:, ATT_WIDTH:2 * ATT_WIDTH], cos, sin_signed, first_half)
        v = hx[:, 2 * ATT_WIDTH:3 * ATT_WIDTH]
        u_ref[0] = hx[:, 3 * ATT_WIDTH:]
    else:
        k = hx[:, :ATT_WIDTH]
        v = hx[:, ATT_WIDTH:]
    for hh in range(ATT_HEADS):
        hs = slice(hh * HEAD_DIM, (hh + 1) * HEAD_DIM)
        k_ref[0, hh] = k[:, hs].astype(bf16)
        vt_ref[0, hh] = v[:, hs].T.astype(bf16)
        if latent:
            qt_ref[0, hh] = q[:, hs].T.astype(bf16)


def _inproj_call(x, mod, row_of_batch, g, w, rope_tabs=None):
    bx, t, d = x.shape
    tm = min(TM_PROJ, t)
    latent = rope_tabs is not None
    in_specs = [
        pl.BlockSpec((1, tm, d), lambda b, i: (b, i, 0)),
        pl.BlockSpec((1, N_MOD, d), lambda b, i: (row_of_batch(b), 0, 0)),
        pl.BlockSpec((1, d), lambda b, i: (0, 0)),
        pl.BlockSpec(w.shape, lambda b, i: (0, 0), pipeline_mode=pl.Buffered(1)),
    ]
    args = [x, mod, g.reshape(1, d), w]
    k_spec = pl.BlockSpec((1, ATT_HEADS, tm, HEAD_DIM), lambda b, i: (b, 0, i, 0))
    k_shape = jax.ShapeDtypeStruct((bx, ATT_HEADS, t, HEAD_DIM), bf16)
    vt_spec = pl.BlockSpec((1, ATT_HEADS, HEAD_DIM, tm), lambda b, i: (b, 0, 0, i))
    vt_shape = jax.ShapeDtypeStruct((bx, ATT_HEADS, HEAD_DIM, t), bf16)
    if latent:
        in_specs += [pl.BlockSpec((tm, HEAD_DIM), lambda b, i: (i, 0))] * 2
        args += list(rope_tabs)
        out_specs = [pl.BlockSpec((1, ATT_HEADS, HEAD_DIM, tm), lambda b, i: (b, 0, 0, i)),
                     k_spec, vt_spec,
                     pl.BlockSpec((1, tm, POOL_WIDTH), lambda b, i: (b, i, 0))]
        out_shape = [jax.ShapeDtypeStruct((bx, ATT_HEADS, HEAD_DIM, t), bf16), k_shape, vt_shape,
                     jax.ShapeDtypeStruct((bx, t, POOL_WIDTH), f32)]
    else:
        out_specs = [k_spec, vt_spec]
        out_shape = [k_shape, vt_shape]
    return pl.pallas_call(
        functools.partial(_inproj_kernel, latent=latent),
        grid=(bx, t // tm),
        in_specs=in_specs,
        out_specs=out_specs,
        out_shape=out_shape,
        compiler_params=_cparams(("parallel", "parallel")),
        name="inproj_lat" if latent else "inproj_ctx",
    )(*args)


def _attn_kernel(lam_ref, gsub_ref, qt_ref, kl_ref, kc_ref, vtl_ref, vtc_ref, o_ref,
                 s_scr, cm_scr, m_scr, l_scr, acc_scr):
    tq = s_scr.shape[2] * HEAD_DIM
    n_lat = kl_ref.shape[2]
    m_scr[...] = jnp.full(m_scr.shape, NEG_BIG, f32)
    l_scr[...] = jnp.zeros(l_scr.shape, f32)
    acc_scr[...] = jnp.zeros(acc_scr.shape, f32)

    n_col = tq // MXU_COLS
    n_chunks = n_lat // TK_ATT
    stages = [(ti, j) for ti in range(QT_ATT) for j in range(n_chunks + 1)]

    def scores(s, mi, t):
        ti, j = stages[s]
        keys = keys_of(j)
        tk = keys.shape[0]
        q_tile = qt_ref[0, 0, :, ti * tq + t * MXU_COLS:ti * tq + (t + 1) * MXU_COLS]
        blank = jnp.zeros((QK_DIM, MXU_COLS), bf16)
        q_tile = jnp.concatenate([q_tile[0:QK_DIM], blank] if mi == 0
                                 else [blank, q_tile[QK_DIM:]], axis=0)
        st = jnp.dot(keys, q_tile, preferred_element_type=f32)
        for c in range(SLABS_PER_TILE):
            s_scr[s % 2, mi, t * SLABS_PER_TILE + c, 0:tk, :] = (
                st[:, c * HEAD_DIM:(c + 1) * HEAD_DIM])
        cm_scr[s % 2, mi, :, t * MXU_COLS:(t + 1) * MXU_COLS] = (
            jnp.max(st.reshape(tk // 8, 8, MXU_COLS), axis=0))

    def softmax_pv(s, mi, t, mid=None):
        ti, j = stages[s]
        slot = s % 2
        vt = values_of(j)
        tk = vt.shape[1]
        ts = slice(t * MXU_COLS, (t + 1) * MXU_COLS)
        m_old = m_scr[ti, mi, :, ts]
        m_new = jnp.maximum(m_old, jnp.max(cm_scr[slot, mi, :, ts], axis=0, keepdims=True))
        m_scr[ti, mi, :, ts] = m_new
        alpha = jnp.exp2(m_old - m_new)
        mb = [jnp.broadcast_to(m_new[:, c * HEAD_DIM:(c + 1) * HEAD_DIM], (MXU_COLS, HEAD_DIM))
              for c in range(SLABS_PER_TILE)]
        pv = None
        psum = None
        n_blocks = tk // MXU_COLS
        for bi in range(n_blocks):
            rs = slice(bi * MXU_COLS, (bi + 1) * MXU_COLS)
            e = [jnp.exp2(s_scr[slot, mi, t * SLABS_PER_TILE + c, rs, :] - mb[c])
                 for c in range(SLABS_PER_TILE)]
            p = jnp.concatenate([ec.astype(bf16) for ec in e], axis=1)
            part = jnp.dot(vt[:, rs], p, preferred_element_type=f32)
            pv = part if pv is None else pv + part
            part = jnp.concatenate(
                [jnp.sum(ec.reshape(MXU_COLS // 8, 8, HEAD_DIM), axis=0) for ec in e], axis=1)
            psum = part if psum is None else psum + part
            if bi == min(MID_BLOCK, n_blocks - 1) and mid is not None:
                mid()
        acc_scr[ti, mi, :, ts] = alpha * acc_scr[ti, mi, :, ts] + pv
        l_scr[ti, mi, :, ts] = alpha * l_scr[ti, mi, :, ts] + psum

    def keys_of(j):
        if j < n_chunks:
            return kl_ref[0, 0, j * TK_ATT:(j + 1) * TK_ATT, :]
        return kc_ref[0, 0]

    def values_of(j):
        if j < n_chunks:
            return vtl_ref[0, 0, :, j * TK_ATT:(j + 1) * TK_ATT]
        return vtc_ref[0, 0]

    def finalize(ti):
        lam_v = lam_ref[...]
        lam = (jnp.exp(jnp.sum(lam_v[0:1] * lam_v[1:2], axis=-1, keepdims=True))
               - jnp.exp(jnp.sum(lam_v[2:3] * lam_v[3:4], axis=-1, keepdims=True))
               + LAMBDA_INIT)
        o1 = acc_scr[ti, 0] / jnp.sum(l_scr[ti, 0], axis=0, keepdims=True)
        o2 = acc_scr[ti, 1] / jnp.sum(l_scr[ti, 1], axis=0, keepdims=True)
        o = o1 - lam * o2
        ms = jnp.mean(o * o, axis=0, keepdims=True)
        o = o * lax.rsqrt(ms + EPS) * (gsub_ref[...] * (1.0 - LAMBDA_INIT))
        o_ref[0, ti * tq:(ti + 1) * tq, :] = o.T.astype(bf16)

    pieces = [(mi, t) for mi in range(2) for t in range(n_col)]
    scores(0, *pieces[0])
    for s in range(len(stages)):
        for pi, (mi, t) in enumerate(pieces):
            todo = []
            if s == 0 and pi + 1 < len(pieces):
                todo.append(functools.partial(scores, 0, *pieces[pi + 1]))
            if s + 1 < len(stages):
                todo.append(functools.partial(scores, s + 1, mi, t))
            softmax_pv(s, mi, t, lambda todo=todo: [f() for f in todo])
        if stages[s][1] == n_chunks:
            finalize(stages[s][0])


def _attn_call(lam_rows, g_sub, qt, k_lat, k_ctx, vt_lat, vt_ctx):
    b, _, _, n = qt.shape
    n_ctx = k_ctx.shape[2]
    tq = TQ_ATT
    tstep = QT_ATT * tq
    return pl.pallas_call(
        _attn_kernel,
        grid=(b, ATT_HEADS, n // tstep),
        in_specs=[
            pl.BlockSpec((4, QK_DIM), lambda bb, h, i: (0, 0)),
            pl.BlockSpec((HEAD_DIM, 1), lambda bb, h, i: (0, 0)),
            pl.BlockSpec((1, 1, HEAD_DIM, tstep), lambda bb, h, i: (bb, h, 0, i)),
            pl.BlockSpec((1, 1, n, HEAD_DIM), lambda bb, h, i: (bb, h, 0, 0)),
            pl.BlockSpec((1, 1, n_ctx, HEAD_DIM), lambda bb, h, i: (bb, h, 0, 0)),
            pl.BlockSpec((1, 1, HEAD_DIM, n), lambda bb, h, i: (bb, h, 0, 0)),
            pl.BlockSpec((1, 1, HEAD_DIM, n_ctx), lambda bb, h, i: (bb, h, 0, 0)),
        ],
        out_specs=pl.BlockSpec((1, tstep, HEAD_DIM), lambda bb, h, i: (bb, i, h)),
        out_shape=jax.ShapeDtypeStruct((b, n, ATT_WIDTH), bf16),
        scratch_shapes=[
            pltpu.VMEM((2, 2, tq // HEAD_DIM, TK_ATT, HEAD_DIM), f32),
            pltpu.VMEM((2, 2, 8, tq), f32),
            pltpu.VMEM((QT_ATT, 2, 1, tq), f32),
            pltpu.VMEM((QT_ATT, 2, 8, tq), f32),
            pltpu.VMEM((QT_ATT, 2, HEAD_DIM, tq), f32),
        ],
        compiler_params=_cparams(("parallel", "parallel", "arbitrary")),
        name="attn",
    )(lam_rows, g_sub.reshape(HEAD_DIM, 1), qt, k_lat, k_ctx, vt_lat, vt_ctx)


def _mixffn_kernel(x_ref, mod_ref, att_ref, u_ref, up_ref, un_ref, wp_ref, ps_ref, wo_ref,
                   g2_ref, win_ref, wout_ref, gf_ref, o_ref, run, *, n_tok):
    i = pl.program_id(1)
    nt = pl.num_programs(1)
    tm = x_ref.shape[1]
    hl = POOL_HALO
    y_att = jnp.dot(att_ref[0], wo_ref[0:ATT_WIDTH, :], preferred_element_type=f32)
    base = 2 * hl
    top = base + tm + hl
    zeros = jnp.zeros((hl, POOL_WIDTH), f32)
    run[0, 0:hl, :] = zeros
    run[0, hl:base, :] = jnp.where(i > 0, up_ref[0], 0.0)
    run[0, base:base + tm, :] = u_ref[0]
    run[0, base + tm:top, :] = jnp.where(i < nt - 1, un_ref[0], 0.0)
    for k in range(1, len(POOL_WINDOWS)):
        reach = 1 << (k - 1)
        lanes = slice((k - 1) * POOL_GROUP_DIM, POOL_WIDTH)
        run[k, 0:hl, :] = zeros
        run[k, hl:top, lanes] = run[k - 1, hl - reach:top - reach, lanes] + run[k - 1, hl:top, lanes]

    t_head = i * tm + lax.broadcasted_iota(jnp.int32, (hl, 1), 0)
    t_tail = t_head + (tm - hl)
    pooled = []
    for gi, w in enumerate(POOL_WINDOWS):
        lo = w // 2
        hi = w - lo - 1
        sl = slice(gi * POOL_GROUP_DIM, (gi + 1) * POOL_GROUP_DIM)
        if gi + 1 < len(POOL_WINDOWS):
            total = run[gi + 1, base + hi:base + hi + tm, sl]
        else:
            half = w // 2
            total = (run[gi, base + hi - half:base + hi - half + tm, sl]
                     + run[gi, base + hi:base + hi + tm, sl])

        def cnt(t):
            return (jnp.minimum(t + hi, n_tok - 1) - jnp.maximum(t - lo, 0) + 1).astype(f32)

        mean = jnp.concatenate([total[0:hl] / cnt(t_head),
                                total[hl:tm - hl] * (1.0 / w),
                                total[tm - hl:tm] / cnt(t_tail)], axis=0)
        diff = (mean - run[0, base:base + tm, sl]).astype(bf16)
        pooled.append(jnp.dot(diff, wp_ref[gi], preferred_element_type=f32))
    pool = (jnp.concatenate(pooled, axis=-1) * ps_ref[...]).astype(bf16)
    y = y_att + jnp.dot(pool, wo_ref[ATT_WIDTH:, :], preferred_element_type=f32)
    x2 = x_ref[0] + mod_ref[0, 5:6, :] * y
    y = _swiglu_rows(x2, g2_ref[...], mod_ref[0, 6:7, :], mod_ref[0, 7:8, :], mod_ref[0, 8:9, :],
                     win_ref, wout_ref)
    ms = jnp.mean(y * y, axis=-1, keepdims=True)
    o_ref[0] = y * lax.rsqrt(ms + EPS) * gf_ref[...]


def _mixffn_call(x, mod, att, u, w_pool, pool_scale, w_out, g2, w2_in, w2_out, g_final):
    b, n, d = x.shape
    tm = TM_MIX
    hb = tm // POOL_HALO
    last_halo_block = n // POOL_HALO - 1
    const2 = lambda bb, i: (0, 0)
    return pl.pallas_call(
        functools.partial(_mixffn_kernel, n_tok=n),
        grid=(b, n // tm),
        in_specs=[
            pl.BlockSpec((1, tm, d), lambda bb, i: (bb, i, 0)),
            pl.BlockSpec((1, N_MOD, d), lambda bb, i: (bb, 0, 0)),
            pl.BlockSpec((1, tm, ATT_WIDTH), lambda bb, i: (bb, i, 0)),
            pl.BlockSpec((1, tm, POOL_WIDTH), lambda bb, i: (bb, i, 0)),
            pl.BlockSpec((1, POOL_HALO, POOL_WIDTH),
                         lambda bb, i: (bb, jnp.maximum(i * hb - 1, 0), 0)),
            pl.BlockSpec((1, POOL_HALO, POOL_WIDTH),
                         lambda bb, i: (bb, jnp.minimum((i + 1) * hb, last_halo_block), 0)),
            pl.BlockSpec(w_pool.shape, lambda bb, i: (0, 0, 0), pipeline_mode=pl.Buffered(1)),
            pl.BlockSpec((1, POOL_WIDTH), const2),
            pl.BlockSpec(w_out.shape, const2, pipeline_mode=pl.Buffered(1)),
            pl.BlockSpec((1, d), const2),
            pl.BlockSpec(w2_in.shape, const2, pipeline_mode=pl.Buffered(1)),
            pl.BlockSpec(w2_out.shape, const2, pipeline_mode=pl.Buffered(1)),
            pl.BlockSpec((1, d), const2),
        ],
        out_specs=pl.BlockSpec((1, tm, d), lambda bb, i: (bb, i, 0)),
        out_shape=jax.ShapeDtypeStruct(x.shape, f32),
        scratch_shapes=[pltpu.VMEM((len(POOL_WINDOWS), tm + 3 * POOL_HALO, POOL_WIDTH), f32)],
        compiler_params=_cparams(("parallel", "parallel")),
        name="mixffn",
    )(x, mod, att, u, u, u, w_pool, pool_scale.reshape(1, POOL_WIDTH), w_out,
      g2.reshape(1, d), w2_in, w2_out, g_final.reshape(1, d))


def _rope_tables(n):
    rows = n // GRID_W
    row = jnp.repeat(jnp.arange(rows, dtype=f32), GRID_W)
    col = jnp.tile(jnp.arange(GRID_W, dtype=f32), rows)
    nf = QK_DIM // 4
    freqs = ROPE_BASE ** (-jnp.arange(nf, dtype=f32) / nf)
    ar = row[:, None] * freqs
    ac = col[:, None] * freqs
    ang = jnp.concatenate([ar, ar, ac, ac], axis=-1)
    sign = jnp.tile(jnp.concatenate([-jnp.ones((nf,), f32), jnp.ones((nf,), f32)]), 2)
    cos = jnp.tile(jnp.cos(ang), (1, 2))
    sin_signed = jnp.tile(jnp.sin(ang) * sign, (1, 2))
    return cos, sin_signed


def kernel(x, c, ctx, c_ctx, w_mod, b_mod, g_ffn1, ffn1_w_in, ffn1_w_out, g_mix, w_in,
           lambda_q1, lambda_k1, lambda_q2, lambda_k2, g_sub, w_pool, pool_scale, w_out,
           g_ffn2, ffn2_w_in, ffn2_w_out, g_final):
    b, n, d = x.shape
    l = 0
    cond = jnp.zeros((MOD_ROWS, d), f32).at[:b].set(c).at[b].set(c_ctx)
    mod = _mod_call(cond, w_mod[l], b_mod[l]).reshape(MOD_ROWS, N_MOD, d)
    lat_row = lambda bb: bb
    ctx_row = lambda bb: b

    w2_in, w2_out = ffn2_w_in[l].astype(bf16), ffn2_w_out[l].astype(bf16)
    w_in_b = w_in[l].astype(bf16)
    w_o_b = w_out[l].astype(bf16)
    w_pool_b = w_pool[l].astype(bf16)

    x1 = _ffn_call(x, mod, lat_row, 0, g_ffn1[l], ffn1_w_in[l], ffn1_w_out[l])
    cx1 = _ffn_call(ctx, mod, ctx_row, 0, g_ffn1[l], ffn1_w_in[l], ffn1_w_out[l])

    qt, k_lat, vt_lat, u = _inproj_call(x1, mod, lat_row, g_mix[l], w_in_b, _rope_tables(n))
    k_ctx, vt_ctx = _inproj_call(cx1, mod, ctx_row, g_mix[l],
                                 w_in_b[:, ATT_WIDTH:3 * ATT_WIDTH])

    lam_rows = jnp.stack([lambda_q1[l], lambda_k1[l], lambda_q2[l], lambda_k2[l]]).astype(f32)
    att = _attn_call(lam_rows, g_sub[l], qt, k_lat, k_ctx, vt_lat, vt_ctx)

    return _mixffn_call(x1, mod, att, u, w_pool_b, pool_scale[l], w_o_b,
                        g_ffn2[l], w2_in, w2_out, g_final)
```
